```python
import functools
import jax
import jax.numpy as jnp
from jax import lax
import numpy as np

D_MODEL = 2048
BATCH = 4
SEQ = 2048
DEPTH = 4
DEC_BATCH = 8
DEC_SEQ = 8
PAST_LEN = 16384
PAGE_SIZE = 128

N_META = 16
BLOCK = 128
SB_HEADS = 8
SB_HEAD_DIM = 128
SB_WIDTH = SB_HEADS * SB_HEAD_DIM
SB_BIAS_HI = -6.0
SB_BIAS_LO = -9.0
RET_HEADS = 4
RET_QK_DIM = 256
RET_V_DIM = 512
RET_QK_WIDTH = RET_HEADS * RET_QK_DIM
RET_V_WIDTH = RET_HEADS * RET_V_DIM
ROPE_BASE = 10000.0
IN_COLS = 3 * SB_WIDTH + 2 * RET_QK_WIDTH + 2 * RET_V_WIDTH
MIX_OUT = SB_WIDTH + RET_V_WIDTH
CONV_WIDTH = 31
CONV_CH = D_MODEL
N_GROUPS = 4
EXPERTS_PER_GROUP = 8
N_EXPERTS = N_GROUPS * EXPERTS_PER_GROUP
TOP_K_EXPERT = 2
EXPERT_HIDDEN = 768
MOE_BLOCK = 128
MOE_BLOCK_SMALL = 8
N_ATT_LAYERS = (DEPTH + 1) // 2
N_CONV_LAYERS = DEPTH // 2
EPS = 1e-6

kernel_name = 'hybrid_stickbreak_retnet_conformer_hmoe_step'


def rms_norm(x, g):
    xf = x.astype(jnp.float32)
    y = xf * lax.rsqrt(jnp.mean(xf * xf, axis=-1, keepdims=True) + EPS)
    return (y * g.astype(jnp.float32)).astype(x.dtype)


def layer_norm(x, g, b):
    xf = x.astype(jnp.float32)
    mu = jnp.mean(xf, axis=-1, keepdims=True)
    var = jnp.mean(jnp.square(xf - mu), axis=-1, keepdims=True)
    y = (xf - mu) * lax.rsqrt(var + EPS) * g.astype(jnp.float32) + b.astype(jnp.float32)
    return y.astype(x.dtype)


def rotary(x, pos):
    half = x.shape[-1] // 2
    inv_freq = ROPE_BASE ** (-jnp.arange(half, dtype=jnp.float32) / half)
    ang = pos.astype(jnp.float32)[:, None] * inv_freq[None, :]
    cos = jnp.cos(ang)[None, :, None, :]
    sin = jnp.sin(ang)[None, :, None, :]
    xf = x.astype(jnp.float32)
    x1, x2 = xf[..., :half], xf[..., half:]
    return jnp.concatenate([x1 * cos - x2 * sin, x1 * sin + x2 * cos], axis=-1).astype(x.dtype)


def stick_breaking_block(q, k, v, bias, q_pos, k_pos):
    z = jnp.einsum('bqhd,bkhd->bhqk', q.astype(jnp.float32), k.astype(jnp.float32)) * (SB_HEAD_DIM ** -0.5)
    z = z + bias.astype(jnp.float32)[None, :, None, None]
    causal = (k_pos[None, :] < q_pos[:, None])[None, None]
    log_not = jnp.where(causal, jax.nn.log_sigmoid(-z), 0.0)
    between = lax.cumsum(log_not, axis=3, reverse=True) - log_not
    weight = jnp.where(causal, jnp.exp(jax.nn.log_sigmoid(z) + between), 0.0)
    return jnp.einsum('bhqk,bkhd->bqhd', weight, v.astype(jnp.float32))


def stick_breaking_prompt(q, k, v, bias):
    t = q.shape[1]
    pos = jnp.arange(t)
    bounds = [(0, N_META)] + [(s, min(s + BLOCK, t)) for s in range(N_META, t, BLOCK)]
    outs = [stick_breaking_block(q[:, a:e], k[:, :e], v[:, :e], bias, pos[a:e], pos[:e]) for a, e in bounds]
    return jnp.concatenate(outs, axis=1)


def stick_breaking_sample(q, k, v, bias, k_past, v_past):
    past = k_past.shape[1]
    n_new = q.shape[1]
    k_all = jnp.concatenate([k_past.astype(k.dtype), k], axis=1)
    v_all = jnp.concatenate([v_past.astype(v.dtype), v], axis=1)
    return stick_breaking_block(q, k_all, v_all, bias, past + jnp.arange(n_new), jnp.arange(past + n_new))


def retention_log_decay():
    return jnp.log1p(-jnp.exp2(-5.0 - jnp.arange(RET_HEADS, dtype=jnp.float32)))


def retention_chunk(state, qkv, log_gamma):
    q, k, v = qkv
    c = q.shape[1]
    idx = jnp.arange(c, dtype=jnp.float32)
    rel = idx[:, None] - idx[None, :]
    decay = jnp.where(rel >= 0, jnp.exp(jnp.maximum(rel, 0.0)[None] * log_gamma[:, None, None]), 0.0)
    scores = jnp.einsum('bihd,bmhd->bhim', q, k) * decay[None]
    o = jnp.einsum('bhim,bmhe->bihe', scores, v)
    q_decay = jnp.exp((idx[:, None] + 1.0) * log_gamma[None, :])[None, :, :, None]
    o = o + jnp.einsum('bihd,bhde->bihe', q, state) * q_decay
    k_decay = jnp.exp((c - 1.0 - idx)[:, None] * log_gamma[None, :])[None, :, :, None]
    new_state = jnp.exp(c * log_gamma)[None, :, None, None] * state + jnp.einsum('bmhd,bmhe->bhde', k * k_decay, v)
    return new_state, o


def retention_prompt(q, k, v):
    b, t, h, _ = q.shape
    log_gamma = retention_log_decay()
    q, k, v = (a.astype(jnp.float32) for a in (q, k, v))
    state0 = jnp.zeros((b, h, RET_QK_DIM, RET_V_DIM), jnp.float32)
    state, o_meta = retention_chunk(state0, (q[:, :N_META], k[:, :N_META], v[:, :N_META]), log_gamma)
    n_chunks = (t - N_META) // BLOCK

    def chunks(a):
        return a[:, N_META:].reshape(b, n_chunks, BLOCK, h, a.shape[-1]).transpose(1, 0, 2, 3, 4)

    state, o_rest = lax.scan(lambda s, c: retention_chunk(s, c, log_gamma), state, (chunks(q), chunks(k), chunks(v)))
    o_rest = o_rest.transpose(1, 0, 2, 3, 4).reshape(b, n_chunks * BLOCK, h, RET_V_DIM)
    return jnp.concatenate([o_meta, o_rest], axis=1), state


def retention_sample(q, k, v, state):
    new_state, o = retention_chunk(state.astype(jnp.float32),
                                   (q.astype(jnp.float32), k.astype(jnp.float32), v.astype(jnp.float32)),
                                   retention_log_decay())
    return o, new_state


def att_mixer(h, pos, w_in, q_gain, k_gain, sb_bias, w_out, sb_read, ret_read):
    b, t, _ = h.shape
    sizes = (SB_WIDTH, SB_WIDTH, SB_WIDTH, RET_QK_WIDTH, RET_QK_WIDTH, RET_V_WIDTH, RET_V_WIDTH)
    cuts = [int(c) for c in np.cumsum(sizes)[:-1]]
    q_a, k_a, v_a, q_r, k_r, v_r, g_r = jnp.split(h @ w_in, cuts, axis=-1)
    q_a = rms_norm(q_a.reshape(b, t, SB_HEADS, SB_HEAD_DIM), q_gain)
    k_a = rms_norm(k_a.reshape(b, t, SB_HEADS, SB_HEAD_DIM), k_gain)
    v_a = v_a.reshape(b, t, SB_HEADS, SB_HEAD_DIM)
    q_r = rotary(q_r.reshape(b, t, RET_HEADS, RET_QK_DIM), pos) * (RET_QK_DIM ** -0.5)
    k_r = rotary(k_r.reshape(b, t, RET_HEADS, RET_QK_DIM), pos)
    v_r = v_r.reshape(b, t, RET_HEADS, RET_V_DIM)
    o_a = sb_read(q_a, k_a, v_a, sb_bias)
    o_r, state_new = ret_read(q_r, k_r, v_r)
    mu = jnp.mean(o_r, axis=-1, keepdims=True)
    var = jnp.mean(jnp.square(o_r - mu), axis=-1, keepdims=True)
    o_r = ((o_r - mu) * lax.rsqrt(var + EPS)).reshape(b, t, RET_V_WIDTH) * jax.nn.silu(g_r.astype(jnp.float32))
    mix = jnp.concatenate([o_a.reshape(b, t, SB_WIDTH), o_r], axis=-1).astype(h.dtype)
    return mix @ w_out, k_a, v_a, state_new


def depthwise_causal_conv(u_ext, w_dw):
    return lax.conv_general_dilated(u_ext, w_dw[:, None, :].astype(u_ext.dtype), (1,), 'VALID',
                                    dimension_numbers=('NWC', 'WIO', 'NWC'),
                                    feature_group_count=u_ext.shape[-1])


def conv_mixer(h, past, w_pw1, b_pw1, w_dw, b_dw, ln_g, ln_b, w_pw2, b_pw2):
    u = h @ w_pw1 + b_pw1
    val, gate = jnp.split(u, 2, axis=-1)
    u = val * jax.nn.sigmoid(gate)
    u_ext = jnp.concatenate([past.astype(u.dtype), u], axis=1)
    c = depthwise_causal_conv(u_ext, w_dw) + b_dw
    c = jax.nn.silu(layer_norm(c, ln_g, ln_b))
    return c @ w_pw2 + b_pw2, u_ext[:, -(CONV_WIDTH - 1):]


def hier_moe(h, router_group, router_expert, w_gate, w_up, w_down):
    b, t, d = h.shape
    n = b * t
    hf = h.reshape(n, d)
    group_prob = jax.nn.softmax((hf @ router_group).astype(jnp.float32), axis=-1)
    g_val, g_idx = lax.top_k(group_prob, 1)
    exp_logits = jnp.einsum('nd,dge->nge', hf, router_expert).astype(jnp.float32)
    exp_logits = exp_logits[jnp.arange(n), g_idx[:, 0]]
    e_val, e_idx = lax.top_k(exp_logits, TOP_K_EXPERT)
    gate = g_val * jax.nn.softmax(e_val, axis=-1)
    expert_id = g_idx * EXPERTS_PER_GROUP + e_idx
    n_assign = n * TOP_K_EXPERT
    blk = MOE_BLOCK if n_assign >= N_EXPERTS * MOE_BLOCK else MOE_BLOCK_SMALL
    cap = (n_assign + N_EXPERTS * (blk - 1) + blk - 1) // blk * blk
    e_flat = expert_id.reshape(n_assign).astype(jnp.int32)
    t_flat = jnp.repeat(jnp.arange(n, dtype=jnp.int32), TOP_K_EXPERT)
    w_flat = gate.reshape(n_assign)
    order = jnp.argsort(e_flat)
    e_sorted = e_flat[order]
    counts = jax.ops.segment_sum(jnp.ones_like(e_flat), e_flat, num_segments=N_EXPERTS)
    starts = jnp.cumsum(counts) - counts
    padded = (counts + blk - 1) // blk * blk
    padded_end = jnp.cumsum(padded)
    dest = padded_end[e_sorted] - padded[e_sorted] + jnp.arange(n_assign, dtype=jnp.int32) - starts[e_sorted]
    tok_pad = jnp.full((cap,), n, jnp.int32).at[dest].set(t_flat[order])
    w_pad = jnp.zeros((cap,), jnp.float32).at[dest].set(w_flat[order])
    n_blocks = cap // blk
    block_expert = jnp.minimum(jnp.searchsorted(padded_end, jnp.arange(n_blocks, dtype=jnp.int32) * blk, side='right'),
                               N_EXPERTS - 1)
    x_blocks = jnp.concatenate([hf, jnp.zeros((1, d), hf.dtype)], axis=0)[tok_pad].reshape(n_blocks, blk, d)

    def run_expert(args):
        xe, e = args
        return (jax.nn.silu(xe @ w_gate[e]) * (xe @ w_up[e])) @ w_down[e]

    y = lax.map(run_expert, (x_blocks, block_expert)).reshape(cap, d)
    y = jax.ops.segment_sum(y * w_pad[:, None].astype(y.dtype), tok_pad, num_segments=n + 1)[:n]
    return y.reshape(b, t, d)


def setup_inputs(seed: int = 0) -> dict:
    key = jax.random.key(seed)
    ks = iter(jax.random.split(key, 40))

    def nrm(shape, scale):
        return jax.random.normal(next(ks), shape, jnp.float32) * scale

    n_pages = PAST_LEN // PAGE_SIZE
    n_used = DEC_BATCH * n_pages
    n_pool = n_used + max(1, n_used // 4)
    res = (2 * DEPTH) ** -0.5
    page_table = jax.random.permutation(next(ks), n_pool)[:n_used].reshape(DEC_BATCH, n_pages).astype(jnp.int32)
    sb_bias = jnp.linspace(SB_BIAS_HI, SB_BIAS_LO, SB_HEADS, dtype=jnp.float32)[None, :] + nrm((N_ATT_LAYERS, SB_HEADS), 0.1)
    return {
        'x_prompt': nrm((BATCH, SEQ, D_MODEL), 1.0),
        'x_sample': nrm((DEC_BATCH, DEC_SEQ, D_MODEL), 1.0),
        'cache_k': nrm((N_ATT_LAYERS, n_pool, PAGE_SIZE, SB_HEADS, SB_HEAD_DIM), 1.0),
        'cache_v': nrm((N_ATT_LAYERS, n_pool, PAGE_SIZE, SB_HEADS, SB_HEAD_DIM), 1.0),
        'state_ret': nrm((N_ATT_LAYERS, DEC_BATCH, RET_HEADS, RET_QK_DIM, RET_V_DIM), 2.0),
        'state_conv': nrm((N_CONV_LAYERS, DEC_BATCH, CONV_WIDTH - 1, CONV_CH), 0.5),
        'page_table': page_table,
        'meta_tokens': nrm((N_META, D_MODEL), 1.0),
        'norm_mix': 1.0 + nrm((DEPTH, D_MODEL), 0.01),
        'norm_ffn': 1.0 + nrm((DEPTH, D_MODEL), 0.01),
        'att_w_in': nrm((N_ATT_LAYERS, D_MODEL, IN_COLS), D_MODEL ** -0.5),
        'att_q_gain': 1.0 + nrm((N_ATT_LAYERS, SB_HEAD_DIM), 0.01),
        'att_k_gain': 1.0 + nrm((N_ATT_LAYERS, SB_HEAD_DIM), 0.01),
        'att_sb_bias': sb_bias,
        'att_w_out': nrm((N_ATT_LAYERS, MIX_OUT, D_MODEL), MIX_OUT ** -0.5 * res),
        'conv_w_pw1': nrm((N_CONV_LAYERS, D_MODEL, 2 * CONV_CH), D_MODEL ** -0.5),
        'conv_b_pw1': nrm((N_CONV_LAYERS, 2 * CONV_CH), 0.01),
        'conv_w_dw': nrm((N_CONV_LAYERS, CONV_WIDTH, CONV_CH), CONV_WIDTH ** -0.5),
        'conv_b_dw': nrm((N_CONV_LAYERS, CONV_CH), 0.01),
        'conv_ln_g': 1.0 + nrm((N_CONV_LAYERS, CONV_CH), 0.01),
        'conv_ln_b': nrm((N_CONV_LAYERS, CONV_CH), 0.01),
        'conv_w_pw2': nrm((N_CONV_LAYERS, CONV_CH, D_MODEL), CONV_CH ** -0.5 * res),
        'conv_b_pw2': nrm((N_CONV_LAYERS, D_MODEL), 0.01),
        'moe_router_group': nrm((DEPTH, D_MODEL, N_GROUPS), D_MODEL ** -0.5),
        'moe_router_expert': nrm((DEPTH, D_MODEL, N_GROUPS, EXPERTS_PER_GROUP), D_MODEL ** -0.5),
        'moe_w_gate': nrm((DEPTH, N_EXPERTS, D_MODEL, EXPERT_HIDDEN), D_MODEL ** -0.5),
        'moe_w_up': nrm((DEPTH, N_EXPERTS, D_MODEL, EXPERT_HIDDEN), D_MODEL ** -0.5),
        'moe_w_down': nrm((DEPTH, N_EXPERTS, EXPERT_HIDDEN, D_MODEL), EXPERT_HIDDEN ** -0.5 * res),
    }


def reference(x_prompt, x_sample, cache_k, cache_v, state_ret, state_conv, page_table,
              meta_tokens, norm_mix, norm_ffn, att_w_in, att_q_gain, att_k_gain, att_sb_bias, att_w_out,
              conv_w_pw1, conv_b_pw1, conv_w_dw, conv_b_dw, conv_ln_g, conv_ln_b, conv_w_pw2, conv_b_pw2,
              moe_router_group, moe_router_expert, moe_w_gate, moe_w_up, moe_w_down):
    b = x_prompt.shape[0]
    db, n_new, _ = x_sample.shape
    past_len = page_table.shape[1] * PAGE_SIZE
    meta = jnp.broadcast_to(meta_tokens.astype(x_prompt.dtype)[None], (b, N_META, D_MODEL))
    xp = jnp.concatenate([meta, x_prompt], axis=1)
    xs = x_sample
    pos_p = jnp.arange(xp.shape[1])
    pos_s = past_len + jnp.arange(n_new)
    k_p_rows, v_p_rows, ret_p, conv_p = [], [], [], []
    k_s_rows, v_s_rows, ret_s, conv_s = [], [], [], []
    for layer in range(DEPTH):
        if layer % 2 == 0:
            a = layer // 2
            w = (att_w_in[a], att_q_gain[a], att_k_gain[a], att_sb_bias[a], att_w_out[a])
            dp, k_p, v_p, s_p = att_mixer(rms_norm(xp, norm_mix[layer]), pos_p, *w,
                                          stick_breaking_prompt, retention_prompt)
            k_past = cache_k[a][page_table].reshape(db, past_len, SB_HEADS, SB_HEAD_DIM)
            v_past = cache_v[a][page_table].reshape(db, past_len, SB_HEADS, SB_HEAD_DIM)
            ds, k_s, v_s, s_s = att_mixer(rms_norm(xs, norm_mix[layer]), pos_s, *w,
                                          functools.partial(stick_breaking_sample, k_past=k_past, v_past=v_past),
                                          functools.partial(retention_sample, state=state_ret[a]))
            k_p_rows.append(k_p)
            v_p_rows.append(v_p)
            ret_p.append(s_p.astype(xp.dtype))
            k_s_rows.append(k_s)
            v_s_rows.append(v_s)
            ret_s.append(s_s.astype(state_ret.dtype))
        else:
            c = layer // 2
            w = (conv_w_pw1[c], conv_b_pw1[c], conv_w_dw[c], conv_b_dw[c], conv_ln_g[c], conv_ln_b[c],
                 conv_w_pw2[c], conv_b_pw2[c])
            dp, cs_p = conv_mixer(rms_norm(xp, norm_mix[layer]),
                                  jnp.zeros((b, CONV_WIDTH - 1, CONV_CH), xp.dtype), *w)
            ds, cs_s = conv_mixer(rms_norm(xs, norm_mix[layer]), state_conv[c], *w)
            conv_p.append(cs_p)
            conv_s.append(cs_s.astype(state_conv.dtype))
        xp = xp + dp
        xs = xs + ds
        wm = (moe_router_group[layer], moe_router_expert[layer], moe_w_gate[layer], moe_w_up[layer], moe_w_down[layer])
        xp = xp + hier_moe(rms_norm(xp, norm_ffn[layer]), *wm)
        xs = xs + hier_moe(rms_norm(xs, norm_ffn[layer]), *wm)
    return (xp[:, N_META:], xs,
            jnp.stack(k_p_rows), jnp.stack(v_p_rows), jnp.stack(ret_p), jnp.stack(conv_p),
            jnp.stack(k_s_rows), jnp.stack(v_s_rows), jnp.stack(ret_s), jnp.stack(conv_s))
```

```python
import functools

import numpy as np
import jax
import jax.numpy as jnp
from jax import lax
from jax.experimental import pallas as pl
from jax.experimental.pallas import tpu as pltpu

F32 = jnp.float32
BF16 = jnp.bfloat16

EPS = 1e-6
BLOCK = 128
ROPE_BASE = 10000.0
TOP_K = 2
LANES = 128
TM = 512
TN = 512
MOE_TM = 256
CONV_TM = 128
CONV_HALO = 32
CONV_LANES = 256
VMEM_LIMIT = 56 * 1024 * 1024

_NT_DIMS = (((1,), (1,)), ((), ()))


def _params(sem, vmem=VMEM_LIMIT):
    return pltpu.CompilerParams(dimension_semantics=sem, vmem_limit_bytes=vmem)


def _store_normed(x_ref, g_ref, xn_ref):
    x = x_ref[...]
    ms = jnp.mean(x * x, axis=-1, keepdims=True)
    xn_ref[...] = (x * lax.rsqrt(ms + EPS) * g_ref[...]).astype(xn_ref.dtype)


def _win_kernel(x_ref, g_ref, w_ref, qg_ref, kg_ref, cos_ref, sin_ref, o_ref, xn_ref, *,
                t_q, t_k, t_rq, t_rk, t_rend, ret_scale):
    j = pl.program_id(1)

    @pl.when(j == 0)
    def _():
        _store_normed(x_ref, g_ref, xn_ref)

    acc = jnp.dot(xn_ref[...], w_ref[...], preferred_element_type=F32)
    tn = acc.shape[1]
    is_qk = j < t_k
    is_rot = (j >= t_rq) & (j < t_rend)

    @pl.when(is_qk)
    def _():
        gain = jnp.where(j < t_q, qg_ref[...], kg_ref[...])
        for g in range(tn // LANES):
            a = acc[:, g * LANES:(g + 1) * LANES]
            ms = jnp.mean(a * a, axis=-1, keepdims=True)
            o_ref[:, g * LANES:(g + 1) * LANES] = a * lax.rsqrt(ms + EPS) * gain

    @pl.when(is_rot)
    def _():
        scale = jnp.where(j < t_rk, ret_scale, 1.0).astype(F32)
        cos = cos_ref[...]
        sin = sin_ref[...]
        half = cos.shape[1]
        for hd in range(tn // (2 * half)):
            lo = hd * 2 * half
            x1 = acc[:, lo:lo + half]
            x2 = acc[:, lo + half:lo + 2 * half]
            o_ref[:, lo:lo + half] = (x1 * cos - x2 * sin) * scale
            o_ref[:, lo + half:lo + 2 * half] = (x1 * sin + x2 * cos) * scale

    @pl.when(jnp.logical_not(is_qk | is_rot))
    def _():
        o_ref[...] = acc


def _w_in_call(x, g, w_bf, q_gain, k_gain, cos_t, sin_t, dims):
    nt, d = x.shape
    n = w_bf.shape[1]
    sb_w, rqk_w = dims["sb_width"], dims["ret_qk_width"]
    half = dims["ret_qk_dim"] // 2
    t_q = sb_w // TN
    t_k = 2 * sb_w // TN
    t_rq = 3 * sb_w // TN
    t_rk = t_rq + rqk_w // TN
    t_rend = t_rk + rqk_w // TN
    kern = functools.partial(_win_kernel, t_q=t_q, t_k=t_k, t_rq=t_rq, t_rk=t_rk, t_rend=t_rend,
                             ret_scale=float(dims["ret_qk_dim"]) ** -0.5)
    return pl.pallas_call(
        kern,
        out_shape=jax.ShapeDtypeStruct((nt, n), F32),
        grid=(nt // TM, n // TN),
        in_specs=[
            pl.BlockSpec((TM, d), lambda i, j: (i, 0)),
            pl.BlockSpec((1, d), lambda i, j: (0, 0)),
            pl.BlockSpec((d, TN), lambda i, j: (0, j)),
            pl.BlockSpec((1, LANES), lambda i, j: (0, 0)),
            pl.BlockSpec((1, LANES), lambda i, j: (0, 0)),
            pl.BlockSpec((TM, half), lambda i, j: (i, 0)),
            pl.BlockSpec((TM, half), lambda i, j: (i, 0)),
        ],
        out_specs=pl.BlockSpec((TM, TN), lambda i, j: (i, j)),
        scratch_shapes=[pltpu.VMEM((TM, d), BF16)],
        compiler_params=_params(("parallel", "arbitrary")),
        name="w_in",
    )(x, g.reshape(1, d), w_bf, q_gain.reshape(1, -1), k_gain.reshape(1, -1), cos_t, sin_t)


def _pw1_kernel(x_ref, g_ref, wv_ref, wg_ref, bv_ref, bg_ref, o_ref, xn_ref):
    @pl.when(pl.program_id(1) == 0)
    def _():
        _store_normed(x_ref, g_ref, xn_ref)

    xn = xn_ref[...]
    val = jnp.dot(xn, wv_ref[...], preferred_element_type=F32) + bv_ref[...]
    gate = jnp.dot(xn, wg_ref[...], preferred_element_type=F32) + bg_ref[...]
    o_ref[...] = val * jax.nn.sigmoid(gate)


def _pw1_call(x, g, w_bf, b):
    nt, d = x.shape
    c = w_bf.shape[1] // 2
    nj = c // TN
    b2 = b.reshape(1, 2 * c)
    return pl.pallas_call(
        _pw1_kernel,
        out_shape=jax.ShapeDtypeStruct((nt, c), F32),
        grid=(nt // TM, nj),
        in_specs=[
            pl.BlockSpec((TM, d), lambda i, j: (i, 0)),
            pl.BlockSpec((1, d), lambda i, j: (0, 0)),
            pl.BlockSpec((d, TN), lambda i, j: (0, j)),
            pl.BlockSpec((d, TN), lambda i, j: (0, j + nj)),
            pl.BlockSpec((1, TN), lambda i, j: (0, j)),
            pl.BlockSpec((1, TN), lambda i, j: (0, j + nj)),
        ],
        out_specs=pl.BlockSpec((TM, TN), lambda i, j: (i, j)),
        scratch_shapes=[pltpu.VMEM((TM, d), BF16)],
        compiler_params=_params(("parallel", "arbitrary")),
        name="pw1_glu",
    )(x, g.reshape(1, d), w_bf, w_bf, b2, b2)


def _mm_res_kernel(*refs, n_lhs):
    main_refs = refs[:n_lhs]
    aux_refs = refs[n_lhs:2 * n_lhs]
    w_refs = refs[2 * n_lhs:3 * n_lhs]
    b_ref, r_ref, o_ref = refs[3 * n_lhs:3 * n_lhs + 3]
    abf_refs = refs[3 * n_lhs + 3:]
    i = pl.program_id(0)
    is_aux = i == pl.num_programs(0) - 1

    @pl.when((pl.program_id(1) == 0) & jnp.logical_not(is_aux))
    def _():
        for a_ref, abf_ref in zip(main_refs, abf_refs):
            abf_ref[...] = a_ref[...].astype(BF16)

    @pl.when((pl.program_id(1) == 0) & is_aux)
    def _():
        for a_ref, abf_ref in zip(aux_refs, abf_refs):
            abf_ref[...] = a_ref[...].astype(BF16)

    acc = jnp.dot(abf_refs[0][...], w_refs[0][...], preferred_element_type=F32)
    for abf_ref, w_ref in zip(abf_refs[1:], w_refs[1:]):
        acc = acc + jnp.dot(abf_ref[...], w_ref[...], preferred_element_type=F32)
    o_ref[...] = r_ref[...] + (acc + b_ref[...])


def _mm_res_call(lhs_list, aux_list, w_list, bias, res, name):
    nt, d = res.shape
    n_lhs = len(lhs_list)
    n_main_tiles = nt // TM - 1
    in_specs = []
    for a in lhs_list:
        in_specs.append(pl.BlockSpec((TM, a.shape[1]), lambda i, j: (jnp.minimum(i, n_main_tiles - 1), 0)))
    for a in aux_list:
        in_specs.append(pl.BlockSpec((TM, a.shape[1]), lambda i, j: (0, 0)))
    for w in w_list:
        in_specs.append(pl.BlockSpec((w.shape[0], TN), lambda i, j: (0, j)))
    in_specs.append(pl.BlockSpec((1, TN), lambda i, j: (0, j)))
    in_specs.append(pl.BlockSpec((TM, TN), lambda i, j: (i, j)))
    return pl.pallas_call(
        functools.partial(_mm_res_kernel, n_lhs=n_lhs),
        out_shape=jax.ShapeDtypeStruct((nt, d), F32),
        grid=(nt // TM, d // TN),
        in_specs=in_specs,
        out_specs=pl.BlockSpec((TM, TN), lambda i, j: (i, j)),
        scratch_shapes=[pltpu.VMEM((TM, a.shape[1]), BF16) for a in lhs_list],
        compiler_params=_params(("parallel", "arbitrary")),
        name=name,
    )(*lhs_list, *aux_list, *w_list, bias.reshape(1, d), res)


def _router_kernel(x_ref, g_ref, w_ref, hn_ref, lg_ref):
    x = x_ref[...]
    ms = jnp.mean(x * x, axis=-1, keepdims=True)
    hn = x * lax.rsqrt(ms + EPS) * g_ref[...]
    hn_ref[...] = hn
    lg_ref[...] = jnp.dot(hn.astype(BF16), w_ref[...].astype(BF16), preferred_element_type=F32)


def _router_call(x, g, w_router):
    nt, d = x.shape
    tm = TM // 2
    return pl.pallas_call(
        _router_kernel,
        out_shape=(jax.ShapeDtypeStruct((nt, d), F32), jax.ShapeDtypeStruct((nt, LANES), F32)),
        grid=(nt // tm,),
        in_specs=[
            pl.BlockSpec((tm, d), lambda i: (i, 0)),
            pl.BlockSpec((1, d), lambda i: (0, 0)),
            pl.BlockSpec((d, LANES), lambda i: (0, 0)),
        ],
        out_specs=(pl.BlockSpec((tm, d), lambda i: (i, 0)), pl.BlockSpec((tm, LANES), lambda i: (i, 0))),
        compiler_params=_params(("parallel",)),
        name="router",
    )(x, g.reshape(1, d), w_router)


def _rev_excl_cumsum(x, tri):
    hi = x.astype(BF16)
    r1 = x - hi.astype(F32)
    mid = r1.astype(BF16)
    lo = (r1 - mid.astype(F32)).astype(BF16)
    out = jnp.dot(hi, tri, preferred_element_type=F32)
    out = out + jnp.dot(mid, tri, preferred_element_type=F32)
    return out + jnp.dot(lo, tri, preferred_element_type=F32)


def _sb_weights(z, mask, carry, tri):
    sp = jnp.maximum(z, 0.0) + jnp.log1p(jnp.exp(-jnp.abs(z)))
    log_not = -sp
    if mask is not None:
        log_not = jnp.where(mask, log_not, 0.0)
    between = _rev_excl_cumsum(log_not, tri) + carry
    w = jnp.exp((z - sp) + between)
    if mask is not None:
        w = jnp.where(mask, w, 0.0)
    return w, carry + jnp.sum(log_not, axis=-1, keepdims=True)


def _attn_prompt_kernel(bias_ref, q_ref, k_ref, v_ref, mk_ref, mv_ref, tri_ref, o_ref, *,
                        n_meta, has_main, scale):
    h = pl.program_id(1)
    i = pl.program_id(2)
    q = q_ref[...].astype(BF16)
    bias = bias_ref[h]
    tri = tri_ref[...]
    nq = q.shape[0]
    row = lax.broadcasted_iota(jnp.int32, (nq, BLOCK), 0)
    col = lax.broadcasted_iota(jnp.int32, (nq, BLOCK), 1)

    def step(k_blk, v_blk, mask, carry, acc):
        z = lax.dot_general(q, k_blk.astype(BF16), _NT_DIMS, preferred_element_type=F32) * scale + bias
        w, carry = _sb_weights(z, mask, carry, tri)
        acc = acc + jnp.dot(w.astype(BF16), v_blk.astype(BF16), preferred_element_type=F32)
        return carry, acc

    carry = jnp.zeros((nq, 1), F32)
    acc = jnp.zeros((nq, q.shape[1]), F32)
    if has_main:
        off = pl.multiple_of(i * BLOCK, BLOCK)
        carry, acc = step(k_ref[pl.ds(off, BLOCK), :], v_ref[pl.ds(off, BLOCK), :], col < row, carry, acc)

        def body(n, ca):
            o2 = pl.multiple_of((i - 1 - n) * BLOCK, BLOCK)
            return step(k_ref[pl.ds(o2, BLOCK), :], v_ref[pl.ds(o2, BLOCK), :], None, *ca)

        carry, acc = lax.fori_loop(0, i, body, (carry, acc))
        meta_mask = col < n_meta
    else:
        meta_mask = (col < n_meta) & (col < row)
    carry, acc = step(mk_ref[...], mv_ref[...], meta_mask, carry, acc)
    o_ref[...] = acc


def _attn_prompt_call(qkv, sb_bias, tri, dims, has_main):
    nt = qkv.shape[0]
    b, seq, h, hd = dims["batch"], dims["seq"], dims["sb_heads"], dims["sb_head_dim"]
    nq = seq // BLOCK
    aux_blk = (b * seq) // BLOCK
    kern = functools.partial(_attn_prompt_kernel, n_meta=dims["n_meta"], has_main=has_main,
                             scale=float(hd) ** -0.5)
    if has_main:
        grid = (b, h, nq)
        q_map = lambda bb, hh, ii: (bb * nq + ii, hh)
        out_rows = b * seq
    else:
        grid = (1, h, 1)
        q_map = lambda bb, hh, ii: (aux_blk, hh)
        out_rows = BLOCK
    o_map = q_map if has_main else (lambda bb, hh, ii: (0, hh))
    return pl.pallas_call(
        kern,
        out_shape=jax.ShapeDtypeStruct((out_rows, h * hd), F32),
        grid=grid,
        in_specs=[
            pl.BlockSpec(memory_space=pltpu.SMEM),
            pl.BlockSpec((BLOCK, hd), q_map),
            pl.BlockSpec((seq, hd), lambda bb, hh, ii: (bb, h + hh)),
            pl.BlockSpec((seq, hd), lambda bb, hh, ii: (bb, 2 * h + hh)),
            pl.BlockSpec((BLOCK, hd), lambda bb, hh, ii: (aux_blk, h + hh)),
            pl.BlockSpec((BLOCK, hd), lambda bb, hh, ii: (aux_blk, 2 * h + hh)),
            pl.BlockSpec((BLOCK, BLOCK), lambda bb, hh, ii: (0, 0)),
        ],
        out_specs=pl.BlockSpec((BLOCK, hd), o_map),
        compiler_params=_params(("parallel", "parallel", "arbitrary")),
        name="attn_prompt" if has_main else "attn_meta",
    )(sb_bias.astype(F32), qkv, qkv, qkv, qkv, qkv, tri)


def _attn_sample_kernel(pt_ref, q_ref, bias_ref, kn_ref, vn_ref, kp_ref, vp_ref, tri_ref, o_ref,
                        carry_ref, acc_ref, *, n_heads, n_new, scale):
    j = pl.program_id(1)
    last = pl.num_programs(1) - 1
    q = q_ref[...]
    tri = tri_ref[...]
    nrow = q.shape[0]

    def process(kf_ref, vf_ref, mask):
        zs = []
        for hh in range(n_heads):
            k_h = kf_ref[pl.ds(hh, BLOCK, stride=n_heads), :].astype(BF16)
            zs.append(lax.dot_general(q[hh * n_new:(hh + 1) * n_new].astype(BF16), k_h, _NT_DIMS,
                                      preferred_element_type=F32))
        z = jnp.concatenate(zs, axis=0) * scale + bias_ref[...]
        w, carry = _sb_weights(z, mask, carry_ref[...], tri)
        carry_ref[...] = carry
        outs = []
        for hh in range(n_heads):
            v_h = vf_ref[pl.ds(hh, BLOCK, stride=n_heads), :].astype(BF16)
            outs.append(jnp.dot(w[hh * n_new:(hh + 1) * n_new].astype(BF16), v_h, preferred_element_type=F32))
        acc_ref[...] += jnp.concatenate(outs, axis=0)

    @pl.when(j == 0)
    def _():
        carry_ref[...] = jnp.zeros_like(carry_ref)
        acc_ref[...] = jnp.zeros_like(acc_ref)
        row = lax.broadcasted_iota(jnp.int32, (nrow, BLOCK), 0)
        col = lax.broadcasted_iota(jnp.int32, (nrow, BLOCK), 1)
        process(kn_ref, vn_ref, col < (row % n_new))

    @pl.when(j > 0)
    def _():
        process(kp_ref, vp_ref, None)

    @pl.when(j == last)
    def _():
        o_ref[...] = acc_ref[...]


def _attn_sample_call(page_table, q_s, bias_col, k_new, v_new, cache_k4, cache_v4, tri, layer, dims):
    db, n_pages = page_table.shape
    h, hd, n_new = dims["sb_heads"], dims["sb_head_dim"], dims["dec_seq"]
    rows = cache_k4.shape[2]
    kern = functools.partial(_attn_sample_kernel, n_heads=h, n_new=n_new, scale=float(hd) ** -0.5)

    def page_map(bb, jj, pt):
        return (layer, pt[bb * n_pages + n_pages - jnp.maximum(jj, 1)], 0, 0)

    gs = pltpu.PrefetchScalarGridSpec(
        num_scalar_prefetch=1,
        grid=(db, n_pages + 1),
        in_specs=[
            pl.BlockSpec((None, h * n_new, hd), lambda bb, jj, pt: (bb, 0, 0)),
            pl.BlockSpec((h * n_new, 1), lambda bb, jj, pt: (0, 0)),
            pl.BlockSpec((None, rows, hd), lambda bb, jj, pt: (bb, 0, 0)),
            pl.BlockSpec((None, rows, hd), lambda bb, jj, pt: (bb, 0, 0)),
            pl.BlockSpec((None, None, rows, hd), page_map),
            pl.BlockSpec((None, None, rows, hd), page_map),
            pl.BlockSpec((BLOCK, BLOCK), lambda bb, jj, pt: (0, 0)),
        ],
        out_specs=pl.BlockSpec((None, h * n_new, hd), lambda bb, jj, pt: (bb, 0, 0)),
        scratch_shapes=[pltpu.VMEM((h * n_new, 1), F32), pltpu.VMEM((h * n_new, hd), F32)],
    )
    return pl.pallas_call(
        kern,
        out_shape=jax.ShapeDtypeStruct((db, h * n_new, hd), F32),
        grid_spec=gs,
        compiler_params=_params(("parallel", "arbitrary")),
        name="attn_sample",
    )(page_table.reshape(-1), q_s, bias_col, k_new, v_new, cache_k4, cache_v4, tri)


def _ret_kernel(q_ref, k_ref, v_ref, g_ref, s0_ref, dec_ref, qd_ref, kd_ref, sd_ref, o_ref, sout_ref, state_ref):
    c = pl.program_id(2)

    @pl.when(c == 0)
    def _():
        state_ref[...] = s0_ref[...]

    q = q_ref[...]
    k = k_ref[...]
    vb = v_ref[...].astype(BF16)
    qb = q.astype(BF16)
    state = state_ref[...]
    scores = lax.dot_general(qb, k.astype(BF16), _NT_DIMS, preferred_element_type=F32) * dec_ref[...]
    o = jnp.dot(scores.astype(BF16), vb, preferred_element_type=F32)
    o = o + jnp.dot(qb, state.astype(BF16), preferred_element_type=F32) * qd_ref[...]
    kd = (k * kd_ref[...]).astype(BF16)
    new_state = sd_ref[...] * state + lax.dot_general(kd, vb, (((0,), (0,)), ((), ())),
                                                      preferred_element_type=F32)
    state_ref[...] = new_state
    mu = jnp.mean(o, axis=-1, keepdims=True)
    var = jnp.mean(jnp.square(o - mu), axis=-1, keepdims=True)
    o_ref[...] = ((o - mu) * lax.rsqrt(var + EPS)) * jax.nn.silu(g_ref[...])

    @pl.when(c == pl.num_programs(2) - 1)
    def _():
        sout_ref[...] = new_state


def _ret_call(q, k, v, g, col_blk, s0, dec, qd, kd, sd, n_seq, n_chunks, rows, out_rows, dims, name,
              s0_per_seq, kd_per_seq):
    nh, dk, dv = dims["ret_heads"], dims["ret_qk_dim"], dims["ret_v_dim"]
    qb, kb, vb, gb = col_blk

    def rmap(off):
        return lambda s, hh, c: (s * n_chunks + c, off + hh)

    s0_map = (lambda s, hh, c: (s, hh, 0, 0)) if s0_per_seq else (lambda s, hh, c: (0, hh, 0, 0))
    kd_map = (lambda s, hh, c: (s, hh, 0, 0)) if kd_per_seq else (lambda s, hh, c: (0, hh, 0, 0))
    return pl.pallas_call(
        _ret_kernel,
        out_shape=(jax.ShapeDtypeStruct((out_rows, nh * dv), F32),
                   jax.ShapeDtypeStruct((n_seq, nh, dk, dv), F32)),
        grid=(n_seq, nh, n_chunks),
        in_specs=[
            pl.BlockSpec((rows, dk), rmap(qb)),
            pl.BlockSpec((rows, dk), rmap(kb)),
            pl.BlockSpec((rows, dv), rmap(vb)),
            pl.BlockSpec((rows, dv), rmap(gb)),
            pl.BlockSpec((None, None, dk, dv), s0_map),
            pl.BlockSpec((None, rows, rows), lambda s, hh, c: (hh, 0, 0)),
            pl.BlockSpec((None, rows, 1), lambda s, hh, c: (hh, 0, 0)),
            pl.BlockSpec((None, None, rows, 1), kd_map),
            pl.BlockSpec((None, None, 1, 1), kd_map),
        ],
        out_specs=(pl.BlockSpec((rows, dv), rmap(0)),
                   pl.BlockSpec((None, None, dk, dv), lambda s, hh, c: (s, hh, 0, 0))),
        scratch_shapes=[pltpu.VMEM((dk, dv), F32)],
        compiler_params=_params(("parallel", "parallel", "arbitrary")),
        name=name,
    )(q, k, v, g, s0, dec, qd, kd, sd)


def _conv_kernel(prev_ref, first_ref, cur_ref, w_ref, b_ref, lg_ref, lb_ref, o_ref, ubuf_ref, *,
                 tiles_per_seq, width):
    tm, ch = cur_ref.shape
    if tiles_per_seq is None:
        halo = first_ref[...]
    else:
        is_first = (pl.program_id(0) % tiles_per_seq) == 0
        halo = jnp.where(is_first, first_ref[...], prev_ref[...])
    ubuf_ref[0:CONV_HALO, :] = halo.astype(BF16).astype(F32)
    ubuf_ref[CONV_HALO:CONV_HALO + tm, :] = cur_ref[...].astype(BF16).astype(F32)
    base = CONV_HALO - (width - 1)

    def chunk(cc, total):
        lo = pl.multiple_of(cc * CONV_LANES, CONV_LANES)
        acc = jnp.zeros((tm, CONV_LANES), F32)
        for kk in range(width):
            acc = acc + w_ref[kk:kk + 1, pl.ds(lo, CONV_LANES)] * ubuf_ref[base + kk:base + kk + tm, pl.ds(lo, CONV_LANES)]
        acc = acc + b_ref[:, pl.ds(lo, CONV_LANES)]
        o_ref[:, pl.ds(lo, CONV_LANES)] = acc
        return total + jnp.sum(acc, axis=-1, keepdims=True)

    total = lax.fori_loop(0, ch // CONV_LANES, chunk, jnp.zeros((tm, 1), F32))
    mu = total / ch
    y = o_ref[...]
    var = jnp.mean(jnp.square(y - mu), axis=-1, keepdims=True)
    yn = (y - mu) * lax.rsqrt(var + EPS) * lg_ref[...] + lb_ref[...]
    o_ref[...] = jax.nn.silu(yn)


def _conv_main_call(u, first, w_dw, b_dw, ln_g, ln_b, dims):
    nt, ch = u.shape
    b, seq = dims["batch"], dims["seq"]
    tiles_per_seq = seq // CONV_TM
    width = w_dw.shape[0]
    ratio = CONV_TM // CONV_HALO
    kern = functools.partial(_conv_kernel, tiles_per_seq=tiles_per_seq, width=width)
    return pl.pallas_call(
        kern,
        out_shape=jax.ShapeDtypeStruct((b * seq, ch), F32),
        grid=(b * tiles_per_seq,),
        in_specs=[
            pl.BlockSpec((CONV_HALO, ch), lambda i: (jnp.maximum(i * ratio - 1, 0), 0)),
            pl.BlockSpec((CONV_HALO, ch), lambda i: (0, 0)),
            pl.BlockSpec((CONV_TM, ch), lambda i: (i, 0)),
            pl.BlockSpec((width, ch), lambda i: (0, 0)),
            pl.BlockSpec((1, ch), lambda i: (0, 0)),
            pl.BlockSpec((1, ch), lambda i: (0, 0)),
            pl.BlockSpec((1, ch), lambda i: (0, 0)),
        ],
        out_specs=pl.BlockSpec((CONV_TM, ch), lambda i: (i, 0)),
        scratch_shapes=[pltpu.VMEM((CONV_HALO + CONV_TM, ch), F32)],
        compiler_params=_params(("parallel",)),
        name="conv_main",
    )(u, first, u, w_dw, b_dw.reshape(1, ch), ln_g.reshape(1, ch), ln_b.reshape(1, ch))


def _conv_aux_call(cur, prev, w_dw, b_dw, ln_g, ln_b):
    n_seq, rows, ch = cur.shape
    width = w_dw.shape[0]
    kern = functools.partial(_conv_kernel, tiles_per_seq=None, width=width)
    return pl.pallas_call(
        kern,
        out_shape=jax.ShapeDtypeStruct((n_seq, rows, ch), F32),
        grid=(n_seq,),
        in_specs=[
            pl.BlockSpec((None, CONV_HALO, ch), lambda i: (i, 0, 0)),
            pl.BlockSpec((None, CONV_HALO, ch), lambda i: (i, 0, 0)),
            pl.BlockSpec((None, rows, ch), lambda i: (i, 0, 0)),
            pl.BlockSpec((width, ch), lambda i: (0, 0)),
            pl.BlockSpec((1, ch), lambda i: (0, 0)),
            pl.BlockSpec((1, ch), lambda i: (0, 0)),
            pl.BlockSpec((1, ch), lambda i: (0, 0)),
        ],
        out_specs=pl.BlockSpec((None, rows, ch), lambda i: (i, 0, 0)),
        scratch_shapes=[pltpu.VMEM((CONV_HALO + rows, ch), F32)],
        compiler_params=_params(("parallel",)),
        name="conv_aux",
    )(prev, prev, cur, w_dw, b_dw.reshape(1, ch), ln_g.reshape(1, ch), ln_b.reshape(1, ch))


def _moe_kernel(te_ref, tok_ref, dst_ref, nu_ref, hn_hbm, wg_ref, wu_ref, wd_ref, gw_ref, y_hbm,
                xbuf, ybuf, gsem, ssem, zsem, *, n_real, n_rows):
    t = pl.program_id(0)
    n_used = nu_ref[0]
    tm = ybuf.shape[0]
    slot = t % 2

    def gather_copy(tile, buf_slot, r):
        tok = tok_ref[tile * tm + r]
        return pltpu.make_async_copy(hn_hbm.at[pl.ds(tok, 1)], xbuf.at[buf_slot, pl.ds(r, 1)], gsem.at[buf_slot])

    def start_gather(tile, buf_slot):
        def body(r, _):
            gather_copy(tile, buf_slot, r).start()
            return 0
        lax.fori_loop(0, tm, body, 0)

    def wait_gather(tile, buf_slot):
        def body(r, _):
            gather_copy(tile, buf_slot, r).wait()
            return 0
        lax.fori_loop(0, tm, body, 0)

    def scatter_copy(tile, r):
        dst = dst_ref[tile * tm + r]
        return dst, pltpu.make_async_copy(ybuf.at[pl.ds(r, 1)], y_hbm.at[pl.ds(jnp.maximum(dst, 0), 1)], ssem)

    def start_scatter(tile):
        def body(r, _):
            dst, cp = scatter_copy(tile, r)

            @pl.when(dst >= 0)
            def _():
                cp.start()
            return 0
        lax.fori_loop(0, tm, body, 0)

    def wait_scatter(tile):
        def body(r, _):
            dst, cp = scatter_copy(tile, r)

            @pl.when(dst >= 0)
            def _():
                cp.wait()
            return 0
        lax.fori_loop(0, tm, body, 0)

    @pl.when(t == 0)
    def _():
        ybuf[...] = jnp.zeros_like(ybuf)
        n_pad = n_rows - n_real
        copies = []
        for s in range(TOP_K):
            done = 0
            while done < n_pad:
                cnt = min(tm, n_pad - done)
                copies.append(pltpu.make_async_copy(ybuf.at[pl.ds(0, cnt)],
                                                    y_hbm.at[pl.ds(s * n_rows + n_real + done, cnt)], zsem))
                done += cnt
        for cp in copies:
            cp.start()
        for cp in copies:
            cp.wait()
        start_gather(0, 0)

    @pl.when(t < n_used)
    def _():
        wait_gather(t, slot)

        @pl.when(t + 1 < n_used)
        def _():
            start_gather(t + 1, 1 - slot)

        x = xbuf[slot].astype(BF16)
        gate = jnp.dot(x, wg_ref[...], preferred_element_type=F32)
        up = jnp.dot(x, wu_ref[...], preferred_element_type=F32)
        hid = (jax.nn.silu(gate) * up).astype(BF16)
        y = jnp.dot(hid, wd_ref[...], preferred_element_type=F32) * gw_ref[...]

        @pl.when(t > 0)
        def _():
            wait_scatter(t - 1)

        ybuf[...] = y
        start_scatter(t)

        @pl.when(t == n_used - 1)
        def _():
            wait_scatter(t)


def _moe_call(hn, tile_expert, tok_pad, dst_pad, n_used, gate_w, wg_bf, wu_bf, wd_bf, layer, n_real):
    nt, d = hn.shape
    cap = tok_pad.shape[0]
    n_tiles = cap // MOE_TM
    hid = wg_bf.shape[-1]
    kern = functools.partial(_moe_kernel, n_real=n_real, n_rows=nt)
    gs = pltpu.PrefetchScalarGridSpec(
        num_scalar_prefetch=4,
        grid=(n_tiles,),
        in_specs=[
            pl.BlockSpec(memory_space=pl.ANY),
            pl.BlockSpec((None, None, d, hid), lambda t, te, *_: (layer, te[t], 0, 0)),
            pl.BlockSpec((None, None, d, hid), lambda t, te, *_: (layer, te[t], 0, 0)),
            pl.BlockSpec((None, None, hid, d), lambda t, te, *_: (layer, te[t], 0, 0)),
            pl.BlockSpec((MOE_TM, 1), lambda t, *_: (t, 0)),
        ],
        out_specs=pl.BlockSpec(memory_space=pl.ANY),
        scratch_shapes=[
            pltpu.VMEM((2, MOE_TM, d), F32),
            pltpu.VMEM((MOE_TM, d), F32),
            pltpu.SemaphoreType.DMA((2,)),
            pltpu.SemaphoreType.DMA(()),
            pltpu.SemaphoreType.DMA(()),
        ],
    )
    return pl.pallas_call(
        kern,
        out_shape=jax.ShapeDtypeStruct((TOP_K * nt, d), F32),
        grid_spec=gs,
        compiler_params=_params(("arbitrary",)),
        name="moe_experts",
    )(tile_expert, tok_pad, dst_pad, n_used, hn, wg_bf, wu_bf, wd_bf, gate_w)


def _combine_kernel(x_ref, y0_ref, y1_ref, o_ref):
    o_ref[...] = x_ref[...] + (y0_ref[...] + y1_ref[...])


def _combine_call(x, y2):
    nt, d = x.shape
    nb = nt // TM
    return pl.pallas_call(
        _combine_kernel,
        out_shape=jax.ShapeDtypeStruct((nt, d), F32),
        grid=(nb,),
        in_specs=[
            pl.BlockSpec((TM, d), lambda i: (i, 0)),
            pl.BlockSpec((TM, d), lambda i: (i, 0)),
            pl.BlockSpec((TM, d), lambda i: (i + nb, 0)),
        ],
        out_specs=pl.BlockSpec((TM, d), lambda i: (i, 0)),
        compiler_params=_params(("parallel",)),
        name="moe_combine",
    )(x, y2, y2)


def _route(logits, n_real, n_groups, per_group, nt):
    n_exp = n_groups * per_group
    lg = logits[:n_real]
    group_prob = jax.nn.softmax(lg[:, :n_groups], axis=-1)
    g_val, g_idx = lax.top_k(group_prob, 1)
    exp_logits = lg[:, n_groups:n_groups + n_exp].reshape(n_real, n_groups, per_group)
    exp_logits = exp_logits[jnp.arange(n_real), g_idx[:, 0]]
    e_val, e_idx = lax.top_k(exp_logits, TOP_K)
    gate = g_val * jax.nn.softmax(e_val, axis=-1)
    expert_id = g_idx * per_group + e_idx
    n_assign = n_real * TOP_K
    blk = MOE_TM
    cap = (n_assign + n_exp * (blk - 1) + blk - 1) // blk * blk
    e_flat = expert_id.reshape(n_assign).astype(jnp.int32)
    t_flat = jnp.repeat(jnp.arange(n_real, dtype=jnp.int32), TOP_K)
    s_flat = jnp.tile(jnp.arange(TOP_K, dtype=jnp.int32), n_real)
    w_flat = gate.reshape(n_assign)
    order = jnp.argsort(e_flat)
    e_sorted = e_flat[order]
    counts = jnp.zeros((n_exp,), jnp.int32).at[e_flat].add(1)
    starts = jnp.cumsum(counts) - counts
    padded = (counts + blk - 1) // blk * blk
    padded_end = jnp.cumsum(padded)
    dest = padded_end[e_sorted] - padded[e_sorted] + jnp.arange(n_assign, dtype=jnp.int32) - starts[e_sorted]
    tok_pad = jnp.full((cap,), n_real, jnp.int32).at[dest].set(t_flat[order])
    dst_pad = jnp.full((cap,), -1, jnp.int32).at[dest].set(s_flat[order] * nt + t_flat[order])
    w_pad = jnp.zeros((cap,), F32).at[dest].set(w_flat[order])
    n_tiles = cap // blk
    tile_expert = jnp.minimum(
        jnp.searchsorted(padded_end, jnp.arange(n_tiles, dtype=jnp.int32) * blk, side="right"),
        n_exp - 1).astype(jnp.int32)
    n_used = (padded_end[-1] // blk).astype(jnp.int32).reshape(1)
    return tile_expert, tok_pad, dst_pad, n_used, w_pad.reshape(cap, 1)


def _moe_layer(x, g, w_router, wg_bf, wu_bf, wd_bf, layer, n_real, n_groups, per_group):
    hn, logits = _router_call(x, g, w_router)
    tile_expert, tok_pad, dst_pad, n_used, gate_w = _route(logits, n_real, n_groups, per_group, x.shape[0])
    y2 = _moe_call(hn, tile_expert, tok_pad, dst_pad, n_used, gate_w, wg_bf, wu_bf, wd_bf, layer, n_real)
    return _combine_call(x, y2)


def _decay_tables(n_heads, rows, live_rows):
    log_gamma = jnp.log1p(-jnp.exp2(-5.0 - jnp.arange(n_heads, dtype=F32)))
    idx = jnp.arange(rows, dtype=F32)
    rel = idx[:, None] - idx[None, :]
    dec = jnp.where(rel >= 0, jnp.exp(jnp.maximum(rel, 0.0)[None] * log_gamma[:, None, None]), 0.0)
    qd = jnp.exp((idx[None, :] + 1.0) * log_gamma[:, None])[:, :, None]
    kd = jnp.exp((live_rows - 1.0 - idx)[None, :] * log_gamma[:, None])
    kd = jnp.where(idx[None, :] < live_rows, kd, 0.0)[:, :, None]
    sd = jnp.exp(live_rows * log_gamma)[:, None, None]
    return dec, qd, kd, sd


def kernel(x_prompt, x_sample, cache_k, cache_v, state_ret, state_conv, page_table, meta_tokens, norm_mix, norm_ffn, att_w_in, att_q_gain, att_k_gain, att_sb_bias, att_w_out, conv_w_pw1, conv_b_pw1, conv_w_dw, conv_b_dw, conv_ln_g, conv_ln_b, conv_w_pw2, conv_b_pw2, moe_router_group, moe_router_expert, moe_w_gate, moe_w_up, moe_w_down):
    b, seq, d = x_prompt.shape
    db, n_new, _ = x_sample.shape
    n_meta = meta_tokens.shape[0]
    depth = norm_mix.shape[0]
    _, n_pool, page, sb_heads, sb_hd = cache_k.shape
    _, _, ret_heads, ret_dk, ret_dv = state_ret.shape
    n_groups, per_group = moe_router_expert.shape[2:]
    past_len = page_table.shape[1] * page
    sb_w = sb_heads * sb_hd
    rqk_w = ret_heads * ret_dk
    rv_w = ret_heads * ret_dv
    conv_w = conv_w_dw.shape[2]
    dims = dict(batch=b, seq=seq, n_meta=n_meta, sb_heads=sb_heads, sb_head_dim=sb_hd, sb_width=sb_w,
                ret_heads=ret_heads, ret_qk_dim=ret_dk, ret_v_dim=ret_dv, ret_qk_width=rqk_w,
                dec_seq=n_new)
    assert seq % TM == 0 and (b * seq) % TM == 0 and n_meta + db * n_new <= BLOCK
    assert page == BLOCK and sb_hd == LANES and ret_dk == 2 * LANES and n_meta % 8 == 0 and n_new == 8

    n_main = b * seq
    nt = n_main + TM
    r_meta = n_main
    r_samp = n_main + n_meta
    n_real = r_samp + db * n_new
    aux_pad = nt - n_real

    x = jnp.concatenate([x_prompt.reshape(n_main, d), meta_tokens.astype(F32), x_sample.reshape(db * n_new, d),
                         jnp.zeros((aux_pad, d), F32)], axis=0)

    pos = jnp.concatenate([jnp.tile(n_meta + jnp.arange(seq), b), jnp.arange(n_meta),
                           jnp.tile(past_len + jnp.arange(n_new), db), jnp.zeros((aux_pad,), jnp.int32)])
    half = ret_dk // 2
    inv_freq = ROPE_BASE ** (-jnp.arange(half, dtype=F32) / half)
    ang = pos.astype(F32)[:, None] * inv_freq[None, :]
    cos_t, sin_t = jnp.cos(ang), jnp.sin(ang)

    idx = jnp.arange(BLOCK)
    tri = (idx[:, None] > idx[None, :]).astype(BF16)

    dec_m, qd_m, kd_m, sd_m = _decay_tables(ret_heads, BLOCK, float(BLOCK))
    aux_rows = n_meta
    dec_a, qd_a, kd_meta, sd_meta = _decay_tables(ret_heads, aux_rows, float(n_meta))
    _, _, kd_s, sd_s = _decay_tables(ret_heads, aux_rows, float(n_new))
    kd_aux = jnp.concatenate([kd_meta[None], jnp.broadcast_to(kd_s[None], (db,) + kd_s.shape)], axis=0)
    sd_aux = jnp.concatenate([sd_meta[None], jnp.broadcast_to(sd_s[None], (db,) + sd_s.shape)], axis=0)

    cache_k4 = cache_k.reshape(cache_k.shape[0], n_pool, page * sb_heads, sb_hd)
    cache_v4 = cache_v.reshape(cache_v.shape[0], n_pool, page * sb_heads, sb_hd)

    moe_wg = moe_w_gate.astype(BF16)
    moe_wu = moe_w_up.astype(BF16)
    moe_wd = moe_w_down.astype(BF16)
    w_router = jnp.concatenate(
        [moe_router_group, moe_router_expert.reshape(depth, d, n_groups * per_group),
         jnp.zeros((depth, d, LANES - n_groups - n_groups * per_group), F32)], axis=-1)

    def pad_aux(rows_list, width):
        used = sum(r.shape[0] for r in rows_list)
        return jnp.concatenate(list(rows_list) + [jnp.zeros((TM - used, width), F32)], axis=0)

    k_p, v_p, ret_p, conv_p, k_s, v_s, ret_s, conv_s = [], [], [], [], [], [], [], []
    for layer in range(depth):
        if layer % 2 == 0:
            a = layer // 2
            qkv = _w_in_call(x, norm_mix[layer], att_w_in[a].astype(BF16), att_q_gain[a], att_k_gain[a],
                             cos_t, sin_t, dims)
            c_k, c_v = sb_w, 2 * sb_w
            c_qr = 3 * sb_w
            c_kr = c_qr + rqk_w
            c_vr = c_kr + rqk_w
            c_gr = c_vr + rv_w
            aux = qkv[n_main:n_main + BLOCK]
            meta_k, meta_v = aux[:n_meta, c_k:c_k + sb_w], aux[:n_meta, c_v:c_v + sb_w]
            samp = aux[n_meta:n_meta + db * n_new]
            samp_k, samp_v = samp[:, c_k:c_k + sb_w], samp[:, c_v:c_v + sb_w]

            o_a = _attn_prompt_call(qkv, att_sb_bias[a], tri, dims, True)
            o_a_meta = _attn_prompt_call(qkv, att_sb_bias[a], tri, dims, False)[:n_meta]
            q_s = samp[:, :sb_w].reshape(db, n_new, sb_heads, sb_hd).transpose(0, 2, 1, 3)
            q_s = q_s.reshape(db, sb_heads * n_new, sb_hd)

            def new_page(rows):
                pg = rows.reshape(db, n_new * sb_heads, sb_hd)
                return jnp.concatenate([pg, jnp.zeros((db, (page - n_new) * sb_heads, sb_hd), F32)], axis=1)

            bias_col = jnp.repeat(att_sb_bias[a].astype(F32), n_new).reshape(sb_heads * n_new, 1)
            o_a_s = _attn_sample_call(page_table, q_s, bias_col, new_page(samp_k), new_page(samp_v),
                                      cache_k4, cache_v4, tri, a, dims)
            o_a_s = o_a_s.reshape(db, sb_heads, n_new, sb_hd).transpose(0, 2, 1, 3).reshape(db * n_new, sb_w)
            o_a_aux = pad_aux([o_a_meta, o_a_s], sb_w)

            def aux_seqs(lo, width):
                m = aux[:n_meta, lo:lo + width]
                m = jnp.concatenate([m, jnp.zeros((aux_rows - n_meta, width), F32)], axis=0)[None]
                s = samp[:, lo:lo + width].reshape(db, n_new, width)
                s = jnp.concatenate([s, jnp.zeros((db, aux_rows - n_new, width), F32)], axis=1)
                return jnp.concatenate([m, s], axis=0).reshape((db + 1) * aux_rows, width)

            s0_aux = jnp.concatenate([jnp.zeros((1, ret_heads, ret_dk, ret_dv), F32), state_ret[a]], axis=0)
            o_r_aux, s_aux = _ret_call(aux_seqs(c_qr, rqk_w), aux_seqs(c_kr, rqk_w), aux_seqs(c_vr, rv_w),
                                       aux_seqs(c_gr, rv_w), (0, 0, 0, 0), s0_aux, dec_a, qd_a, kd_aux, sd_aux,
                                       db + 1, 1, aux_rows, (db + 1) * aux_rows, dims, "ret_aux", True, True)
            o_r_aux = o_r_aux.reshape(db + 1, aux_rows, rv_w)
            o_r, s_main = _ret_call(qkv, qkv, qkv, qkv,
                                    (c_qr // ret_dk, c_kr // ret_dk, c_vr // ret_dv, c_gr // ret_dv),
                                    s_aux[0:1], dec_m, qd_m, kd_m[None], sd_m[None],
                                    b, seq // BLOCK, BLOCK, n_main, dims, "ret_prompt", False, False)
            o_r_aux = pad_aux([o_r_aux[0, :n_meta], o_r_aux[1:, :n_new].reshape(db * n_new, rv_w)], rv_w)

            w_out = att_w_out[a].astype(BF16)
            x = _mm_res_call([o_a, o_r], [o_a_aux, o_r_aux], [w_out[:sb_w], w_out[sb_w:]],
                             jnp.zeros((d,), F32), x, "w_out")

            def with_meta(main_cols, meta_rows):
                m = jnp.broadcast_to(meta_rows.reshape(1, n_meta, sb_heads, sb_hd), (b, n_meta, sb_heads, sb_hd))
                return jnp.concatenate([m, main_cols.reshape(b, seq, sb_heads, sb_hd)], axis=1)

            k_p.append(with_meta(qkv[:n_main, c_k:c_k + sb_w], meta_k))
            v_p.append(with_meta(qkv[:n_main, c_v:c_v + sb_w], meta_v))
            ret_p.append(s_main)
            k_s.append(samp_k.reshape(db, n_new, sb_heads, sb_hd))
            v_s.append(samp_v.reshape(db, n_new, sb_heads, sb_hd))
            ret_s.append(s_aux[1:])
        else:
            c = layer // 2
            u = _pw1_call(x, norm_mix[layer], conv_w_pw1[c].astype(BF16), conv_b_pw1[c])
            u_meta = u[r_meta:r_meta + n_meta]
            u_samp = u[r_samp:r_samp + db * n_new].reshape(db, n_new, conv_w)
            first = jnp.concatenate([jnp.zeros((CONV_HALO - n_meta, conv_w), F32), u_meta], axis=0)
            w_dw = conv_w_dw[c].astype(BF16).astype(F32)
            cm = _conv_main_call(u, first, w_dw, conv_b_dw[c], conv_ln_g[c], conv_ln_b[c], dims)
            cur_aux = jnp.concatenate(
                [u_meta[None], jnp.concatenate([u_samp, jnp.zeros((db, n_meta - n_new, conv_w), F32)], axis=1)],
                axis=0)
            hist = state_conv[c].astype(F32)
            prev_aux = jnp.concatenate(
                [jnp.zeros((1, CONV_HALO, conv_w), F32),
                 jnp.concatenate([jnp.zeros((db, CONV_HALO - hist.shape[1], conv_w), F32), hist], axis=1)], axis=0)
            ca = _conv_aux_call(cur_aux, prev_aux, w_dw, conv_b_dw[c], conv_ln_g[c], conv_ln_b[c])
            cm_aux = pad_aux([ca[0], ca[1:, :n_new].reshape(db * n_new, conv_w)], conv_w)
            x = _mm_res_call([cm], [cm_aux], [conv_w_pw2[c].astype(BF16)], conv_b_pw2[c], x, "pw2")
            keep = hist.shape[1]
            conv_p.append(u[:n_main].reshape(b, seq, conv_w)[:, seq - keep:])
            conv_s.append(jnp.concatenate([hist, u_samp], axis=1)[:, n_new:].astype(state_conv.dtype))
        x = _moe_layer(x, norm_ffn[layer], w_router[layer], moe_wg, moe_wu, moe_wd, layer, n_real,
                       n_groups, per_group)

    return (x[:n_main].reshape(b, seq, d), x[r_samp:r_samp + db * n_new].reshape(db, n_new, d),
            jnp.stack(k_p), jnp.stack(v_p), jnp.stack(ret_p), jnp.stack(conv_p),
            jnp.stack(k_s), jnp.stack(v_s), jnp.stack(ret_s), jnp.stack(conv_s))
```

```python
import functools

import numpy as np
import jax
import jax.numpy as jnp
from jax import lax
from jax.experimental import pallas as pl
from jax.experimental.pallas import tpu as pltpu

F32 = jnp.float32
BF16 = jnp.bfloat16

EPS = 1e-6
BLOCK = 128
ROPE_BASE = 10000.0
TOP_K = 2
LANES = 128
TM = 512
TN = 512
ATT_HPS = 4
ATT_SPS = 4
MOE_TM = 256
DMA_UNROLL = 8
CONV_TM = 128
CONV_HALO = 32
CONV_LANES = 256
VMEM_LIMIT = 56 * 1024 * 1024

_NT_DIMS = (((1,), (1,)), ((), ()))


def _params(sem, vmem=VMEM_LIMIT):
    return pltpu.CompilerParams(dimension_semantics=sem, vmem_limit_bytes=vmem)


def _store_normed(x_ref, g_ref, xn_ref):
    x = x_ref[...]
    ms = jnp.mean(x * x, axis=-1, keepdims=True)
    xn_ref[...] = (x * lax.rsqrt(ms + EPS) * g_ref[...]).astype(xn_ref.dtype)


def _win_kernel(x_ref, g_ref, w_ref, qg_ref, kg_ref, cos_ref, sin_ref, o_ref, xn_ref, *,
                t_q, t_k, t_rq, t_rk, t_rend, ret_scale):
    j = pl.program_id(1)

    @pl.when(j == 0)
    def _():
        _store_normed(x_ref, g_ref, xn_ref)

    acc = jnp.dot(xn_ref[...], w_ref[...], preferred_element_type=F32)
    tn = acc.shape[1]
    is_qk = j < t_k
    is_rot = (j >= t_rq) & (j < t_rend)

    @pl.when(is_qk)
    def _():
        gain = jnp.where(j < t_q, qg_ref[...], kg_ref[...])
        for g in range(tn // LANES):
            a = acc[:, g * LANES:(g + 1) * LANES]
            ms = jnp.mean(a * a, axis=-1, keepdims=True)
            o_ref[:, g * LANES:(g + 1) * LANES] = a * lax.rsqrt(ms + EPS) * gain

    @pl.when(is_rot)
    def _():
        scale = jnp.where(j < t_rk, ret_scale, 1.0).astype(F32)
        cos = cos_ref[...]
        sin = sin_ref[...]
        half = cos.shape[1]
        for hd in range(tn // (2 * half)):
            lo = hd * 2 * half
            x1 = acc[:, lo:lo + half]
            x2 = acc[:, lo + half:lo + 2 * half]
            o_ref[:, lo:lo + half] = (x1 * cos - x2 * sin) * scale
            o_ref[:, lo + half:lo + 2 * half] = (x1 * sin + x2 * cos) * scale

    @pl.when(jnp.logical_not(is_qk | is_rot))
    def _():
        o_ref[...] = acc


def _w_in_call(x, g, w_bf, q_gain, k_gain, cos_t, sin_t, dims):
    nt, d = x.shape
    n = w_bf.shape[1]
    sb_w, rqk_w = dims["sb_width"], dims["ret_qk_width"]
    half = dims["ret_qk_dim"] // 2
    t_q = sb_w // TN
    t_k = 2 * sb_w // TN
    t_rq = 3 * sb_w // TN
    t_rk = t_rq + rqk_w // TN
    t_rend = t_rk + rqk_w // TN
    kern = functools.partial(_win_kernel, t_q=t_q, t_k=t_k, t_rq=t_rq, t_rk=t_rk, t_rend=t_rend,
                             ret_scale=float(dims["ret_qk_dim"]) ** -0.5)
    return pl.pallas_call(
        kern,
        out_shape=jax.ShapeDtypeStruct((nt, n), F32),
        grid=(nt // TM, n // TN),
        in_specs=[
            pl.BlockSpec((TM, d), lambda i, j: (i, 0)),
            pl.BlockSpec((1, d), lambda i, j: (0, 0)),
            pl.BlockSpec((d, TN), lambda i, j: (0, j)),
            pl.BlockSpec((1, LANES), lambda i, j: (0, 0)),
            pl.BlockSpec((1, LANES), lambda i, j: (0, 0)),
            pl.BlockSpec((TM, half), lambda i, j: (i, 0)),
            pl.BlockSpec((TM, half), lambda i, j: (i, 0)),
        ],
        out_specs=pl.BlockSpec((TM, TN), lambda i, j: (i, j)),
        scratch_shapes=[pltpu.VMEM((TM, d), BF16)],
        compiler_params=_params(("parallel", "arbitrary")),
        name="w_in",
    )(x, g.reshape(1, d), w_bf, q_gain.reshape(1, -1), k_gain.reshape(1, -1), cos_t, sin_t)


def _pw1_kernel(x_ref, g_ref, wv_ref, wg_ref, bv_ref, bg_ref, o_ref, xn_ref):
    @pl.when(pl.program_id(1) == 0)
    def _():
        _store_normed(x_ref, g_ref, xn_ref)

    xn = xn_ref[...]
    val = jnp.dot(xn, wv_ref[...], preferred_element_type=F32) + bv_ref[...]
    gate = jnp.dot(xn, wg_ref[...], preferred_element_type=F32) + bg_ref[...]
    o_ref[...] = val * jax.nn.sigmoid(gate)


def _pw1_call(x, g, w_bf, b):
    nt, d = x.shape
    c = w_bf.shape[1] // 2
    nj = c // TN
    b2 = b.reshape(1, 2 * c)
    return pl.pallas_call(
        _pw1_kernel,
        out_shape=jax.ShapeDtypeStruct((nt, c), F32),
        grid=(nt // TM, nj),
        in_specs=[
            pl.BlockSpec((TM, d), lambda i, j: (i, 0)),
            pl.BlockSpec((1, d), lambda i, j: (0, 0)),
            pl.BlockSpec((d, TN), lambda i, j: (0, j)),
            pl.BlockSpec((d, TN), lambda i, j: (0, j + nj)),
            pl.BlockSpec((1, TN), lambda i, j: (0, j)),
            pl.BlockSpec((1, TN), lambda i, j: (0, j + nj)),
        ],
        out_specs=pl.BlockSpec((TM, TN), lambda i, j: (i, j)),
        scratch_shapes=[pltpu.VMEM((TM, d), BF16)],
        compiler_params=_params(("parallel", "arbitrary")),
        name="pw1_glu",
    )(x, g.reshape(1, d), w_bf, w_bf, b2, b2)


def _mm_res_kernel(*refs, n_lhs):
    main_refs = refs[:n_lhs]
    aux_refs = refs[n_lhs:2 * n_lhs]
    w_refs = refs[2 * n_lhs:3 * n_lhs]
    b_ref, r_ref, o_ref = refs[3 * n_lhs:3 * n_lhs + 3]
    abf_refs = refs[3 * n_lhs + 3:]
    i = pl.program_id(0)
    is_aux = i == pl.num_programs(0) - 1

    @pl.when((pl.program_id(1) == 0) & jnp.logical_not(is_aux))
    def _():
        for a_ref, abf_ref in zip(main_refs, abf_refs):
            abf_ref[...] = a_ref[...].astype(BF16)

    @pl.when((pl.program_id(1) == 0) & is_aux)
    def _():
        for a_ref, abf_ref in zip(aux_refs, abf_refs):
            abf_ref[...] = a_ref[...].astype(BF16)

    acc = jnp.dot(abf_refs[0][...], w_refs[0][...], preferred_element_type=F32)
    for abf_ref, w_ref in zip(abf_refs[1:], w_refs[1:]):
        acc = acc + jnp.dot(abf_ref[...], w_ref[...], preferred_element_type=F32)
    o_ref[...] = r_ref[...] + (acc + b_ref[...])


def _mm_res_call(lhs_list, aux_list, w_list, bias, res, name):
    nt, d = res.shape
    n_lhs = len(lhs_list)
    n_main_tiles = nt // TM - 1
    in_specs = []
    for a in lhs_list:
        in_specs.append(pl.BlockSpec((TM, a.shape[1]), lambda i, j: (jnp.minimum(i, n_main_tiles - 1), 0)))
    for a in aux_list:
        in_specs.append(pl.BlockSpec((TM, a.shape[1]), lambda i, j: (0, 0)))
    for w in w_list:
        in_specs.append(pl.BlockSpec((w.shape[0], TN), lambda i, j: (0, j)))
    in_specs.append(pl.BlockSpec((1, TN), lambda i, j: (0, j)))
    in_specs.append(pl.BlockSpec((TM, TN), lambda i, j: (i, j)))
    return pl.pallas_call(
        functools.partial(_mm_res_kernel, n_lhs=n_lhs),
        out_shape=jax.ShapeDtypeStruct((nt, d), F32),
        grid=(nt // TM, d // TN),
        in_specs=in_specs,
        out_specs=pl.BlockSpec((TM, TN), lambda i, j: (i, j)),
        scratch_shapes=[pltpu.VMEM((TM, a.shape[1]), BF16) for a in lhs_list],
        compiler_params=_params(("parallel", "arbitrary")),
        name=name,
    )(*lhs_list, *aux_list, *w_list, bias.reshape(1, d), res)


def _router_kernel(x_ref, g_ref, w_ref, hn_ref, ids_ref, gates_ref, *, n_groups, per_group):
    x = x_ref[...]
    ms = jnp.mean(x * x, axis=-1, keepdims=True)
    hn = x * lax.rsqrt(ms + EPS) * g_ref[...]
    hn_ref[...] = hn
    lg = jnp.dot(hn.astype(BF16), w_ref[...].astype(BF16), preferred_element_type=F32)
    lane = lax.broadcasted_iota(jnp.int32, lg.shape, 1).astype(F32)
    neg = -jnp.inf
    far = float(LANES)

    def first_argmax(v, vmax):
        return jnp.min(jnp.where(v == vmax, lane, far), axis=-1, keepdims=True)

    gl = jnp.where(lane < n_groups, lg, neg)
    p = jnp.exp(gl - jnp.max(gl, axis=-1, keepdims=True))
    prob = p / jnp.sum(p, axis=-1, keepdims=True)
    g_val = jnp.max(prob, axis=-1, keepdims=True)
    g_idx = first_argmax(prob, g_val)
    lo = n_groups + g_idx * per_group
    el = jnp.where((lane >= lo) & (lane < lo + per_group), lg, neg)
    e1 = jnp.max(el, axis=-1, keepdims=True)
    i1 = first_argmax(el, e1)
    el2 = jnp.where(lane == i1, neg, el)
    e2 = jnp.max(el2, axis=-1, keepdims=True)
    i2 = first_argmax(el2, e2)
    t = jnp.exp(e2 - e1)
    den = 1.0 + t
    ids_ref[...] = jnp.where(lane == 0, i1 - n_groups, jnp.where(lane == 1, i2 - n_groups, 0.0)).astype(jnp.int32)
    gates_ref[...] = jnp.where(lane == 0, g_val * (1.0 / den), jnp.where(lane == 1, g_val * (t / den), 0.0))


def _router_call(x, g, w_router, n_groups, per_group):
    nt, d = x.shape
    tm = TM // 2
    return pl.pallas_call(
        functools.partial(_router_kernel, n_groups=n_groups, per_group=per_group),
        out_shape=(jax.ShapeDtypeStruct((nt, d), F32), jax.ShapeDtypeStruct((nt, LANES), jnp.int32),
                   jax.ShapeDtypeStruct((nt, LANES), F32)),
        grid=(nt // tm,),
        in_specs=[
            pl.BlockSpec((tm, d), lambda i: (i, 0)),
            pl.BlockSpec((1, d), lambda i: (0, 0)),
            pl.BlockSpec((d, LANES), lambda i: (0, 0)),
        ],
        out_specs=(pl.BlockSpec((tm, d), lambda i: (i, 0)), pl.BlockSpec((tm, LANES), lambda i: (i, 0)),
                   pl.BlockSpec((tm, LANES), lambda i: (i, 0))),
        compiler_params=_params(("parallel",)),
        name="router",
    )(x, g.reshape(1, d), w_router)


def _sb_weights_many(zs, mask, carries, tri):
    sps = [jnp.maximum(z, 0.0) + jnp.log1p(jnp.exp(-jnp.abs(z))) for z in zs]
    log_nots = [-sp for sp in sps]
    if mask is not None:
        log_nots = [jnp.where(mask, ln, 0.0) for ln in log_nots]
    his = [ln.astype(BF16) for ln in log_nots]
    r1s = [ln - hi.astype(F32) for ln, hi in zip(log_nots, his)]
    mids = [r1.astype(BF16) for r1 in r1s]
    los = [(r1 - mid.astype(F32)).astype(BF16) for r1, mid in zip(r1s, mids)]
    c_hi = [jnp.dot(p, tri, preferred_element_type=F32) for p in his]
    c_mid = [jnp.dot(p, tri, preferred_element_type=F32) for p in mids]
    c_lo = [jnp.dot(p, tri, preferred_element_type=F32) for p in los]
    ws = []
    for z, sp, a, b, c, carry in zip(zs, sps, c_hi, c_mid, c_lo, carries):
        w = jnp.exp((z - sp) + (((a + b) + c) + carry))
        ws.append(w if mask is None else jnp.where(mask, w, 0.0))
    new_carries = [carry + jnp.sum(ln, axis=-1, keepdims=True) for carry, ln in zip(carries, log_nots)]
    return ws, new_carries


def _attn_prompt_kernel(bias_ref, q_ref, k_ref, v_ref, mk_ref, mv_ref, tri_ref, o_ref, carry_ref, acc_ref, *,
                        n_meta, has_main, scale, hd):
    hg = pl.program_id(1)
    i = pl.program_id(2)
    tri = tri_ref[...]
    nq = q_ref.shape[0]
    n_h = q_ref.shape[1] // hd
    row = lax.broadcasted_iota(jnp.int32, (nq, BLOCK), 0)
    col = lax.broadcasted_iota(jnp.int32, (nq, BLOCK), 1)
    qs = [q_ref[:, j * hd:(j + 1) * hd].astype(BF16) for j in range(n_h)]
    biases = [bias_ref[hg * n_h + j] for j in range(n_h)]

    def step(kv_rows, kr, vr, mask):
        zs = [lax.dot_general(qs[j], kr[kv_rows, j * hd:(j + 1) * hd].astype(BF16), _NT_DIMS,
                              preferred_element_type=F32) * scale + biases[j] for j in range(n_h)]
        ws, carries = _sb_weights_many(zs, mask, [carry_ref[j] for j in range(n_h)], tri)
        pvs = [jnp.dot(ws[j].astype(BF16), vr[kv_rows, j * hd:(j + 1) * hd].astype(BF16),
                       preferred_element_type=F32) for j in range(n_h)]
        for j in range(n_h):
            carry_ref[j] = carries[j]
            acc_ref[j] += pvs[j]

    carry_ref[...] = jnp.zeros_like(carry_ref)
    acc_ref[...] = jnp.zeros_like(acc_ref)
    if has_main:
        step(pl.ds(pl.multiple_of(i * BLOCK, BLOCK), BLOCK), k_ref, v_ref, col < row)

        def body(n, _):
            step(pl.ds(pl.multiple_of((i - 1 - n) * BLOCK, BLOCK), BLOCK), k_ref, v_ref, None)
            return 0

        lax.fori_loop(0, i, body, 0)
        meta_mask = col < n_meta
    else:
        meta_mask = (col < n_meta) & (col < row)
    step(slice(None), mk_ref, mv_ref, meta_mask)
    for j in range(n_h):
        o_ref[:, j * hd:(j + 1) * hd] = acc_ref[j]


def _attn_prompt_call(qkv, sb_bias, tri, dims, has_main):
    b, seq, h, hd = dims["batch"], dims["seq"], dims["sb_heads"], dims["sb_head_dim"]
    nq = seq // BLOCK
    aux_blk = (b * seq) // BLOCK
    hw = ATT_HPS * hd
    n_hg = h // ATT_HPS
    kern = functools.partial(_attn_prompt_kernel, n_meta=dims["n_meta"], has_main=has_main,
                             scale=float(hd) ** -0.5, hd=hd)
    if has_main:
        grid = (b, n_hg, nq)
        q_map = lambda bb, hh, ii: (bb * nq + ii, hh)
        out_rows = b * seq
    else:
        grid = (1, n_hg, 1)
        q_map = lambda bb, hh, ii: (aux_blk, hh)
        out_rows = BLOCK
    o_map = q_map if has_main else (lambda bb, hh, ii: (0, hh))
    return pl.pallas_call(
        kern,
        out_shape=jax.ShapeDtypeStruct((out_rows, h * hd), F32),
        grid=grid,
        in_specs=[
            pl.BlockSpec(memory_space=pltpu.SMEM),
            pl.BlockSpec((BLOCK, hw), q_map),
            pl.BlockSpec((seq, hw), lambda bb, hh, ii: (bb, n_hg + hh)),
            pl.BlockSpec((seq, hw), lambda bb, hh, ii: (bb, 2 * n_hg + hh)),
            pl.BlockSpec((BLOCK, hw), lambda bb, hh, ii: (aux_blk, n_hg + hh)),
            pl.BlockSpec((BLOCK, hw), lambda bb, hh, ii: (aux_blk, 2 * n_hg + hh)),
            pl.BlockSpec((BLOCK, BLOCK), lambda bb, hh, ii: (0, 0)),
        ],
        out_specs=pl.BlockSpec((BLOCK, hw), o_map),
        scratch_shapes=[pltpu.VMEM((ATT_HPS, BLOCK, 1), F32), pltpu.VMEM((ATT_HPS, BLOCK, hd), F32)],
        compiler_params=_params(("parallel", "parallel", "arbitrary")),
        name="attn_prompt" if has_main else "attn_meta",
    )(sb_bias.astype(F32), qkv, qkv, qkv, qkv, qkv, tri)


def _attn_sample_kernel(pt_ref, q_ref, bias_ref, kn_ref, vn_ref, *rest, n_heads, n_new, n_seq, scale):
    kp_refs = rest[:n_seq]
    vp_refs = rest[n_seq:2 * n_seq]
    tri_ref, o_ref, carry_ref, acc_ref = rest[2 * n_seq:]
    j = pl.program_id(1)
    last = pl.num_programs(1) - 1
    tri = tri_ref[...]
    nrow = q_ref.shape[1]

    def process(kf_refs, vf_refs, mask):
        zs = []
        for s in range(n_seq):
            q = q_ref[s]
            zh = []
            for hh in range(n_heads):
                k_h = kf_refs[s][pl.ds(hh, BLOCK, stride=n_heads), :].astype(BF16)
                zh.append(lax.dot_general(q[hh * n_new:(hh + 1) * n_new].astype(BF16), k_h, _NT_DIMS,
                                          preferred_element_type=F32))
            zs.append(jnp.concatenate(zh, axis=0) * scale + bias_ref[...])
        ws, carries = _sb_weights_many(zs, mask, [carry_ref[s] for s in range(n_seq)], tri)
        for s in range(n_seq):
            outs = []
            for hh in range(n_heads):
                v_h = vf_refs[s][pl.ds(hh, BLOCK, stride=n_heads), :].astype(BF16)
                outs.append(jnp.dot(ws[s][hh * n_new:(hh + 1) * n_new].astype(BF16), v_h,
                                    preferred_element_type=F32))
            carry_ref[s] = carries[s]
            acc_ref[s] += jnp.concatenate(outs, axis=0)

    @pl.when(j == 0)
    def _():
        carry_ref[...] = jnp.zeros_like(carry_ref)
        acc_ref[...] = jnp.zeros_like(acc_ref)
        row = lax.broadcasted_iota(jnp.int32, (nrow, BLOCK), 0)
        col = lax.broadcasted_iota(jnp.int32, (nrow, BLOCK), 1)
        process([kn_ref.at[s] for s in range(n_seq)], [vn_ref.at[s] for s in range(n_seq)],
                col < (row % n_new))

    @pl.when(j > 0)
    def _():
        process(kp_refs, vp_refs, None)

    @pl.when(j == last)
    def _():
        o_ref[...] = acc_ref[...]


def _attn_sample_call(page_table, q_s, bias_col, k_new, v_new, cache_k4, cache_v4, tri, layer, dims):
    db, n_pages = page_table.shape
    h, hd, n_new = dims["sb_heads"], dims["sb_head_dim"], dims["dec_seq"]
    rows = cache_k4.shape[2]
    n_seq = ATT_SPS if db % ATT_SPS == 0 else 1
    kern = functools.partial(_attn_sample_kernel, n_heads=h, n_new=n_new, n_seq=n_seq, scale=float(hd) ** -0.5)

    def page_map(s):
        return lambda g, jj, pt: (layer, pt[(g * n_seq + s) * n_pages + n_pages - jnp.maximum(jj, 1)], 0, 0)

    page_specs = [pl.BlockSpec((None, None, rows, hd), page_map(s)) for s in range(n_seq)]
    gs = pltpu.PrefetchScalarGridSpec(
        num_scalar_prefetch=1,
        grid=(db // n_seq, n_pages + 1),
        in_specs=[
            pl.BlockSpec((n_seq, h * n_new, hd), lambda g, jj, pt: (g, 0, 0)),
            pl.BlockSpec((h * n_new, 1), lambda g, jj, pt: (0, 0)),
            pl.BlockSpec((n_seq, rows, hd), lambda g, jj, pt: (g, 0, 0)),
            pl.BlockSpec((n_seq, rows, hd), lambda g, jj, pt: (g, 0, 0)),
            *page_specs, *page_specs,
            pl.BlockSpec((BLOCK, BLOCK), lambda g, jj, pt: (0, 0)),
        ],
        out_specs=pl.BlockSpec((n_seq, h * n_new, hd), lambda g, jj, pt: (g, 0, 0)),
        scratch_shapes=[pltpu.VMEM((n_seq, h * n_new, 1), F32), pltpu.VMEM((n_seq, h * n_new, hd), F32)],
    )
    return pl.pallas_call(
        kern,
        out_shape=jax.ShapeDtypeStruct((db, h * n_new, hd), F32),
        grid_spec=gs,
        compiler_params=_params(("parallel", "arbitrary")),
        name="attn_sample",
    )(page_table.reshape(-1), q_s, bias_col, k_new, v_new, *([cache_k4] * n_seq), *([cache_v4] * n_seq), tri)


def _ret_kernel(q_ref, k_ref, v_ref, g_ref, s0_ref, dec_ref, qd_ref, kd_ref, sd_ref, o_ref, sout_ref, state_ref):
    c = pl.program_id(2)

    @pl.when(c == 0)
    def _():
        state_ref[...] = s0_ref[...]

    q = q_ref[...]
    k = k_ref[...]
    vb = v_ref[...].astype(BF16)
    qb = q.astype(BF16)
    state = state_ref[...]
    scores = lax.dot_general(qb, k.astype(BF16), _NT_DIMS, preferred_element_type=F32) * dec_ref[...]
    o = jnp.dot(scores.astype(BF16), vb, preferred_element_type=F32)
    o = o + jnp.dot(qb, state.astype(BF16), preferred_element_type=F32) * qd_ref[...]
    kd = (k * kd_ref[...]).astype(BF16)
    new_state = sd_ref[...] * state + lax.dot_general(kd, vb, (((0,), (0,)), ((), ())),
                                                      preferred_element_type=F32)
    state_ref[...] = new_state
    mu = jnp.mean(o, axis=-1, keepdims=True)
    var = jnp.mean(jnp.square(o - mu), axis=-1, keepdims=True)
    o_ref[...] = ((o - mu) * lax.rsqrt(var + EPS)) * jax.nn.silu(g_ref[...])

    @pl.when(c == pl.num_programs(2) - 1)
    def _():
        sout_ref[...] = new_state


def _ret_call(q, k, v, g, col_blk, s0, dec, qd, kd, sd, n_seq, n_chunks, rows, out_rows, dims, name,
              s0_per_seq, kd_per_seq):
    nh, dk, dv = dims["ret_heads"], dims["ret_qk_dim"], dims["ret_v_dim"]
    qb, kb, vb, gb = col_blk

    def rmap(off):
        return lambda s, hh, c: (s * n_chunks + c, off + hh)

    s0_map = (lambda s, hh, c: (s, hh, 0, 0)) if s0_per_seq else (lambda s, hh, c: (0, hh, 0, 0))
    kd_map = (lambda s, hh, c: (s, hh, 0, 0)) if kd_per_seq else (lambda s, hh, c: (0, hh, 0, 0))
    return pl.pallas_call(
        _ret_kernel,
        out_shape=(jax.ShapeDtypeStruct((out_rows, nh * dv), F32),
                   jax.ShapeDtypeStruct((n_seq, nh, dk, dv), F32)),
        grid=(n_seq, nh, n_chunks),
        in_specs=[
            pl.BlockSpec((rows, dk), rmap(qb)),
            pl.BlockSpec((rows, dk), rmap(kb)),
            pl.BlockSpec((rows, dv), rmap(vb)),
            pl.BlockSpec((rows, dv), rmap(gb)),
            pl.BlockSpec((None, None, dk, dv), s0_map),
            pl.BlockSpec((None, rows, rows), lambda s, hh, c: (hh, 0, 0)),
            pl.BlockSpec((None, rows, 1), lambda s, hh, c: (hh, 0, 0)),
            pl.BlockSpec((None, None, rows, 1), kd_map),
            pl.BlockSpec((None, None, 1, 1), kd_map),
        ],
        out_specs=(pl.BlockSpec((rows, dv), rmap(0)),
                   pl.BlockSpec((None, None, dk, dv), lambda s, hh, c: (s, hh, 0, 0))),
        scratch_shapes=[pltpu.VMEM((dk, dv), F32)],
        compiler_params=_params(("parallel", "parallel", "arbitrary")),
        name=name,
    )(q, k, v, g, s0, dec, qd, kd, sd)


def _conv_kernel(prev_ref, first_ref, cur_ref, w_ref, b_ref, lg_ref, lb_ref, o_ref, ubuf_ref, *,
                 tiles_per_seq, width):
    tm, ch = cur_ref.shape
    if tiles_per_seq is None:
        halo = first_ref[...]
    else:
        is_first = (pl.program_id(0) % tiles_per_seq) == 0
        halo = jnp.where(is_first, first_ref[...], prev_ref[...])
    ubuf_ref[0:CONV_HALO, :] = halo.astype(BF16).astype(F32)
    ubuf_ref[CONV_HALO:CONV_HALO + tm, :] = cur_ref[...].astype(BF16).astype(F32)
    base = CONV_HALO - (width - 1)

    def chunk(cc, total):
        lo = pl.multiple_of(cc * CONV_LANES, CONV_LANES)
        acc = jnp.zeros((tm, CONV_LANES), F32)
        for kk in range(width):
            acc = acc + w_ref[kk:kk + 1, pl.ds(lo, CONV_LANES)] * ubuf_ref[base + kk:base + kk + tm, pl.ds(lo, CONV_LANES)]
        acc = acc + b_ref[:, pl.ds(lo, CONV_LANES)]
        o_ref[:, pl.ds(lo, CONV_LANES)] = acc
        return total + jnp.sum(acc, axis=-1, keepdims=True)

    total = lax.fori_loop(0, ch // CONV_LANES, chunk, jnp.zeros((tm, 1), F32))
    mu = total / ch
    y = o_ref[...]
    var = jnp.mean(jnp.square(y - mu), axis=-1, keepdims=True)
    yn = (y - mu) * lax.rsqrt(var + EPS) * lg_ref[...] + lb_ref[...]
    o_ref[...] = jax.nn.silu(yn)


def _conv_main_call(u, first, w_dw, b_dw, ln_g, ln_b, dims):
    nt, ch = u.shape
    b, seq = dims["batch"], dims["seq"]
    tiles_per_seq = seq // CONV_TM
    width = w_dw.shape[0]
    ratio = CONV_TM // CONV_HALO
    kern = functools.partial(_conv_kernel, tiles_per_seq=tiles_per_seq, width=width)
    return pl.pallas_call(
        kern,
        out_shape=jax.ShapeDtypeStruct((b * seq, ch), F32),
        grid=(b * tiles_per_seq,),
        in_specs=[
            pl.BlockSpec((CONV_HALO, ch), lambda i: (jnp.maximum(i * ratio - 1, 0), 0)),
            pl.BlockSpec((CONV_HALO, ch), lambda i: (0, 0)),
            pl.BlockSpec((CONV_TM, ch), lambda i: (i, 0)),
            pl.BlockSpec((width, ch), lambda i: (0, 0)),
            pl.BlockSpec((1, ch), lambda i: (0, 0)),
            pl.BlockSpec((1, ch), lambda i: (0, 0)),
            pl.BlockSpec((1, ch), lambda i: (0, 0)),
        ],
        out_specs=pl.BlockSpec((CONV_TM, ch), lambda i: (i, 0)),
        scratch_shapes=[pltpu.VMEM((CONV_HALO + CONV_TM, ch), F32)],
        compiler_params=_params(("parallel",)),
        name="conv_main",
    )(u, first, u, w_dw, b_dw.reshape(1, ch), ln_g.reshape(1, ch), ln_b.reshape(1, ch))


def _conv_aux_call(cur, prev, w_dw, b_dw, ln_g, ln_b):
    n_seq, rows, ch = cur.shape
    width = w_dw.shape[0]
    kern = functools.partial(_conv_kernel, tiles_per_seq=None, width=width)
    return pl.pallas_call(
        kern,
        out_shape=jax.ShapeDtypeStruct((n_seq, rows, ch), F32),
        grid=(n_seq,),
        in_specs=[
            pl.BlockSpec((None, CONV_HALO, ch), lambda i: (i, 0, 0)),
            pl.BlockSpec((None, CONV_HALO, ch), lambda i: (i, 0, 0)),
            pl.BlockSpec((None, rows, ch), lambda i: (i, 0, 0)),
            pl.BlockSpec((width, ch), lambda i: (0, 0)),
            pl.BlockSpec((1, ch), lambda i: (0, 0)),
            pl.BlockSpec((1, ch), lambda i: (0, 0)),
            pl.BlockSpec((1, ch), lambda i: (0, 0)),
        ],
        out_specs=pl.BlockSpec((None, rows, ch), lambda i: (i, 0, 0)),
        scratch_shapes=[pltpu.VMEM((CONV_HALO + rows, ch), F32)],
        compiler_params=_params(("parallel",)),
        name="conv_aux",
    )(prev, prev, cur, w_dw, b_dw.reshape(1, ch), ln_g.reshape(1, ch), ln_b.reshape(1, ch))


def _moe_kernel(te_ref, tok_ref, dst_ref, nv_ref, nu_ref, hn_hbm, wg_ref, wu_ref, wd_ref, gw_ref, y_hbm,
                xbuf, ybuf, gsem, ssem, zsem, *, n_real, n_rows):
    t = pl.program_id(0)
    n_used = nu_ref[0]
    tm = ybuf.shape[0]
    slot = t % 2

    def gather_copy(tile, buf_slot, r):
        tok = tok_ref[tile * tm + r]
        return pltpu.make_async_copy(hn_hbm.at[pl.ds(tok, 1)], xbuf.at[buf_slot, pl.ds(r, 1)], gsem.at[buf_slot])

    def for_rows(n, fn):
        def group(gi, _):
            for u in range(DMA_UNROLL):
                fn(gi * DMA_UNROLL + u)
            return 0
        lax.fori_loop(0, n // DMA_UNROLL, group, 0)

        def rest(r, _):
            fn(r)
            return 0
        lax.fori_loop((n // DMA_UNROLL) * DMA_UNROLL, n, rest, 0)

    def start_gather(tile, buf_slot):
        for_rows(tm, lambda r: gather_copy(tile, buf_slot, r).start())

    def wait_gather(buf_slot):
        pltpu.make_async_copy(hn_hbm.at[pl.ds(0, tm)], xbuf.at[buf_slot], gsem.at[buf_slot]).wait()

    def scatter_copy(tile, r):
        return pltpu.make_async_copy(ybuf.at[pl.ds(r, 1)], y_hbm.at[pl.ds(dst_ref[tile * tm + r], 1)], ssem)

    def start_scatter(tile):
        for_rows(nv_ref[tile], lambda r: scatter_copy(tile, r).start())

    def wait_scatter(tile):
        n = nv_ref[tile]

        def group(gi, _):
            pltpu.make_async_copy(ybuf.at[pl.ds(0, DMA_UNROLL)], y_hbm.at[pl.ds(0, DMA_UNROLL)], ssem).wait()
            return 0
        lax.fori_loop(0, n // DMA_UNROLL, group, 0)

        def rest(r, _):
            scatter_copy(tile, r).wait()
            return 0
        lax.fori_loop((n // DMA_UNROLL) * DMA_UNROLL, n, rest, 0)

    @pl.when(t == 0)
    def _():
        ybuf[...] = jnp.zeros_like(ybuf)
        n_pad = n_rows - n_real
        copies = []
        for s in range(TOP_K):
            done = 0
            while done < n_pad:
                cnt = min(tm, n_pad - done)
                copies.append(pltpu.make_async_copy(ybuf.at[pl.ds(0, cnt)],
                                                    y_hbm.at[pl.ds(s * n_rows + n_real + done, cnt)], zsem))
                done += cnt
        for cp in copies:
            cp.start()
        for cp in copies:
            cp.wait()
        start_gather(0, 0)

    @pl.when(t < n_used)
    def _():
        wait_gather(slot)

        @pl.when(t + 1 < n_used)
        def _():
            start_gather(t + 1, 1 - slot)

        x = xbuf[slot].astype(BF16)
        gate = jnp.dot(x, wg_ref[...], preferred_element_type=F32)
        up = jnp.dot(x, wu_ref[...], preferred_element_type=F32)
        hid = (jax.nn.silu(gate) * up).astype(BF16)
        y = jnp.dot(hid, wd_ref[...], preferred_element_type=F32) * gw_ref[...]

        @pl.when(t > 0)
        def _():
            wait_scatter(t - 1)

        ybuf[...] = y
        start_scatter(t)

        @pl.when(t == n_used - 1)
        def _():
            wait_scatter(t)


def _moe_call(hn, tile_expert, tok_pad, dst_pad, n_valid, n_used, gate_w, wg_bf, wu_bf, wd_bf, layer, n_real):
    nt, d = hn.shape
    cap = tok_pad.shape[0]
    n_tiles = cap // MOE_TM
    hid = wg_bf.shape[-1]
    kern = functools.partial(_moe_kernel, n_real=n_real, n_rows=nt)
    gs = pltpu.PrefetchScalarGridSpec(
        num_scalar_prefetch=5,
        grid=(n_tiles,),
        in_specs=[
            pl.BlockSpec(memory_space=pl.ANY),
            pl.BlockSpec((None, None, d, hid), lambda t, te, *_: (layer, te[t], 0, 0)),
            pl.BlockSpec((None, None, d, hid), lambda t, te, *_: (layer, te[t], 0, 0)),
            pl.BlockSpec((None, None, hid, d), lambda t, te, *_: (layer, te[t], 0, 0)),
            pl.BlockSpec((MOE_TM, 1), lambda t, *_: (t, 0)),
        ],
        out_specs=pl.BlockSpec(memory_space=pl.ANY),
        scratch_shapes=[
            pltpu.VMEM((2, MOE_TM, d), F32),
            pltpu.VMEM((MOE_TM, d), F32),
            pltpu.SemaphoreType.DMA((2,)),
            pltpu.SemaphoreType.DMA(()),
            pltpu.SemaphoreType.DMA(()),
        ],
    )
    return pl.pallas_call(
        kern,
        out_shape=jax.ShapeDtypeStruct((TOP_K * nt, d), F32),
        grid_spec=gs,
        compiler_params=_params(("arbitrary",)),
        name="moe_experts",
    )(tile_expert, tok_pad, dst_pad, n_valid, n_used, hn, wg_bf, wu_bf, wd_bf, gate_w)


def _combine_kernel(x_ref, y0_ref, y1_ref, o_ref):
    o_ref[...] = x_ref[...] + (y0_ref[...] + y1_ref[...])


def _combine_call(x, y2):
    nt, d = x.shape
    nb = nt // TM
    return pl.pallas_call(
        _combine_kernel,
        out_shape=jax.ShapeDtypeStruct((nt, d), F32),
        grid=(nb,),
        in_specs=[
            pl.BlockSpec((TM, d), lambda i: (i, 0)),
            pl.BlockSpec((TM, d), lambda i: (i, 0)),
            pl.BlockSpec((TM, d), lambda i: (i + nb, 0)),
        ],
        out_specs=pl.BlockSpec((TM, d), lambda i: (i, 0)),
        compiler_params=_params(("parallel",)),
        name="moe_combine",
    )(x, y2, y2)


def _route(ids, gates, n_real, n_exp, nt):
    n_assign = n_real * TOP_K
    blk = MOE_TM
    n_fill = n_exp * (blk - 1)
    cap = (n_assign + n_fill + blk - 1) // blk * blk
    e_flat = ids[:n_real, :TOP_K].reshape(n_assign)
    w_flat = gates[:n_real, :TOP_K].reshape(n_assign)
    t_flat = jnp.repeat(jnp.arange(n_real, dtype=jnp.int32), TOP_K)
    s_flat = jnp.tile(jnp.arange(TOP_K, dtype=jnp.int32), n_real)
    counts = jnp.sum((e_flat[:, None] == jnp.arange(n_exp, dtype=jnp.int32)[None, :]).astype(jnp.int32), axis=0)
    padded = (counts + blk - 1) // blk * blk
    fill_key = jnp.where(jnp.arange(blk - 1, dtype=jnp.int32)[None, :] < (padded - counts)[:, None],
                         jnp.arange(n_exp, dtype=jnp.int32)[:, None], n_exp).reshape(n_fill)
    n_tail = cap - n_assign - n_fill
    keys = jnp.concatenate([e_flat, fill_key, jnp.full((n_tail,), n_exp, jnp.int32)])
    n_extra = n_fill + n_tail
    tok = jnp.concatenate([t_flat, jnp.full((n_extra,), n_real, jnp.int32)])
    dst = jnp.concatenate([s_flat * nt + t_flat, jnp.full((n_extra,), -1, jnp.int32)])
    wgt = jnp.concatenate([w_flat, jnp.zeros((n_extra,), F32)])
    keys_s, tok_pad, dst_pad, w_pad = lax.sort((keys, tok, dst, wgt), num_keys=1, is_stable=True)
    n_tiles = cap // blk
    tile_expert = jnp.minimum(keys_s[::blk], n_exp - 1)
    n_valid = jnp.sum((dst_pad >= 0).astype(jnp.int32).reshape(n_tiles, blk), axis=1)
    n_used = (jnp.sum(padded) // blk).astype(jnp.int32).reshape(1)
    return tile_expert, tok_pad, dst_pad, n_valid, n_used, w_pad.reshape(cap, 1)


def _moe_layer(x, g, w_router, wg_bf, wu_bf, wd_bf, layer, n_real, n_groups, per_group):
    hn, ids, gates = _router_call(x, g, w_router, n_groups, per_group)
    tile_expert, tok_pad, dst_pad, n_valid, n_used, gate_w = _route(ids, gates, n_real, n_groups * per_group,
                                                                   x.shape[0])
    y2 = _moe_call(hn, tile_expert, tok_pad, dst_pad, n_valid, n_used, gate_w, wg_bf, wu_bf, wd_bf, layer, n_real)
    return _combine_call(x, y2)


def _decay_tables(n_heads, rows, live_rows):
    log_gamma = jnp.log1p(-jnp.exp2(-5.0 - jnp.arange(n_heads, dtype=F32)))
    idx = jnp.arange(rows, dtype=F32)
    rel = idx[:, None] - idx[None, :]
    dec = jnp.where(rel >= 0, jnp.exp(jnp.maximum(rel, 0.0)[None] * log_gamma[:, None, None]), 0.0)
    qd = jnp.exp((idx[None, :] + 1.0) * log_gamma[:, None])[:, :, None]
    kd = jnp.exp((live_rows - 1.0 - idx)[None, :] * log_gamma[:, None])
    kd = jnp.where(idx[None, :] < live_rows, kd, 0.0)[:, :, None]
    sd = jnp.exp(live_rows * log_gamma)[:, None, None]
    return dec, qd, kd, sd


def kernel(x_prompt, x_sample, cache_k, cache_v, state_ret, state_conv, page_table, meta_tokens, norm_mix, norm_ffn, att_w_in, att_q_gain, att_k_gain, att_sb_bias, att_w_out, conv_w_pw1, conv_b_pw1, conv_w_dw, conv_b_dw, conv_ln_g, conv_ln_b, conv_w_pw2, conv_b_pw2, moe_router_group, moe_router_expert, moe_w_gate, moe_w_up, moe_w_down):
    b, seq, d = x_prompt.shape
    db, n_new, _ = x_sample.shape
    n_meta = meta_tokens.shape[0]
    depth = norm_mix.shape[0]
    _, n_pool, page, sb_heads, sb_hd = cache_k.shape
    _, _, ret_heads, ret_dk, ret_dv = state_ret.shape
    n_groups, per_group = moe_router_expert.shape[2:]
    past_len = page_table.shape[1] * page
    sb_w = sb_heads * sb_hd
    rqk_w = ret_heads * ret_dk
    rv_w = ret_heads * ret_dv
    conv_w = conv_w_dw.shape[2]
    dims = dict(batch=b, seq=seq, n_meta=n_meta, sb_heads=sb_heads, sb_head_dim=sb_hd, sb_width=sb_w,
                ret_heads=ret_heads, ret_qk_dim=ret_dk, ret_v_dim=ret_dv, ret_qk_width=rqk_w,
                dec_seq=n_new)
    assert seq % TM == 0 and (b * seq) % TM == 0 and n_meta + db * n_new <= BLOCK
    assert page == BLOCK and sb_hd == LANES and ret_dk == 2 * LANES and n_meta % 8 == 0 and n_new == 8

    n_main = b * seq
    nt = n_main + TM
    r_meta = n_main
    r_samp = n_main + n_meta
    n_real = r_samp + db * n_new
    aux_pad = nt - n_real

    x = jnp.concatenate([x_prompt.reshape(n_main, d), meta_tokens.astype(F32), x_sample.reshape(db * n_new, d),
                         jnp.zeros((aux_pad, d), F32)], axis=0)

    pos = jnp.concatenate([jnp.tile(n_meta + jnp.arange(seq), b), jnp.arange(n_meta),
                           jnp.tile(past_len + jnp.arange(n_new), db), jnp.zeros((aux_pad,), jnp.int32)])
    half = ret_dk // 2
    inv_freq = ROPE_BASE ** (-jnp.arange(half, dtype=F32) / half)
    ang = pos.astype(F32)[:, None] * inv_freq[None, :]
    cos_t, sin_t = jnp.cos(ang), jnp.sin(ang)

    idx = jnp.arange(BLOCK)
    tri = (idx[:, None] > idx[None, :]).astype(BF16)

    dec_m, qd_m, kd_m, sd_m = _decay_tables(ret_heads, BLOCK, float(BLOCK))
    aux_rows = n_meta
    dec_a, qd_a, kd_meta, sd_meta = _decay_tables(ret_heads, aux_rows, float(n_meta))
    _, _, kd_s, sd_s = _decay_tables(ret_heads, aux_rows, float(n_new))
    kd_aux = jnp.concatenate([kd_meta[None], jnp.broadcast_to(kd_s[None], (db,) + kd_s.shape)], axis=0)
    sd_aux = jnp.concatenate([sd_meta[None], jnp.broadcast_to(sd_s[None], (db,) + sd_s.shape)], axis=0)

    cache_k4 = cache_k.reshape(cache_k.shape[0], n_pool, page * sb_heads, sb_hd)
    cache_v4 = cache_v.reshape(cache_v.shape[0], n_pool, page * sb_heads, sb_hd)

    moe_wg = moe_w_gate.astype(BF16)
    moe_wu = moe_w_up.astype(BF16)
    moe_wd = moe_w_down.astype(BF16)
    w_router = jnp.concatenate(
        [moe_router_group, moe_router_expert.reshape(depth, d, n_groups * per_group),
         jnp.zeros((depth, d, LANES - n_groups - n_groups * per_group), F32)], axis=-1)

    def pad_aux(rows_list, width):
        used = sum(r.shape[0] for r in rows_list)
        return jnp.concatenate(list(rows_list) + [jnp.zeros((TM - used, width), F32)], axis=0)

    k_p, v_p, ret_p, conv_p, k_s, v_s, ret_s, conv_s = [], [], [], [], [], [], [], []
    for layer in range(depth):
        if layer % 2 == 0:
            a = layer // 2
            qkv = _w_in_call(x, norm_mix[layer], att_w_in[a].astype(BF16), att_q_gain[a], att_k_gain[a],
                             cos_t, sin_t, dims)
            c_k, c_v = sb_w, 2 * sb_w
            c_qr = 3 * sb_w
            c_kr = c_qr + rqk_w
            c_vr = c_kr + rqk_w
            c_gr = c_vr + rv_w
            aux = qkv[n_main:n_main + BLOCK]
            meta_k, meta_v = aux[:n_meta, c_k:c_k + sb_w], aux[:n_meta, c_v:c_v + sb_w]
            samp = aux[n_meta:n_meta + db * n_new]
            samp_k, samp_v = samp[:, c_k:c_k + sb_w], samp[:, c_v:c_v + sb_w]

            o_a = _attn_prompt_call(qkv, att_sb_bias[a], tri, dims, True)
            o_a_meta = _attn_prompt_call(qkv, att_sb_bias[a], tri, dims, False)[:n_meta]
            q_s = samp[:, :sb_w].reshape(db, n_new, sb_heads, sb_hd).transpose(0, 2, 1, 3)
            q_s = q_s.reshape(db, sb_heads * n_new, sb_hd)

            def new_page(rows):
                pg = rows.reshape(db, n_new * sb_heads, sb_hd)
                return jnp.concatenate([pg, jnp.zeros((db, (page - n_new) * sb_heads, sb_hd), F32)], axis=1)

            bias_col = jnp.repeat(att_sb_bias[a].astype(F32), n_new).reshape(sb_heads * n_new, 1)
            o_a_s = _attn_sample_call(page_table, q_s, bias_col, new_page(samp_k), new_page(samp_v),
                                      cache_k4, cache_v4, tri, a, dims)
            o_a_s = o_a_s.reshape(db, sb_heads, n_new, sb_hd).transpose(0, 2, 1, 3).reshape(db * n_new, sb_w)
            o_a_aux = pad_aux([o_a_meta, o_a_s], sb_w)

            def aux_seqs(lo, width):
                m = aux[:n_meta, lo:lo + width]
                m = jnp.concatenate([m, jnp.zeros((aux_rows - n_meta, width), F32)], axis=0)[None]
                s = samp[:, lo:lo + width].reshape(db, n_new, width)
                s = jnp.concatenate([s, jnp.zeros((db, aux_rows - n_new, width), F32)], axis=1)
                return jnp.concatenate([m, s], axis=0).reshape((db + 1) * aux_rows, width)

            s0_aux = jnp.concatenate([jnp.zeros((1, ret_heads, ret_dk, ret_dv), F32), state_ret[a]], axis=0)
            o_r_aux, s_aux = _ret_call(aux_seqs(c_qr, rqk_w), aux_seqs(c_kr, rqk_w), aux_seqs(c_vr, rv_w),
                                       aux_seqs(c_gr, rv_w), (0, 0, 0, 0), s0_aux, dec_a, qd_a, kd_aux, sd_aux,
                                       db + 1, 1, aux_rows, (db + 1) * aux_rows, dims, "ret_aux", True, True)
            o_r_aux = o_r_aux.reshape(db + 1, aux_rows, rv_w)
            o_r, s_main = _ret_call(qkv, qkv, qkv, qkv,
                                    (c_qr // ret_dk, c_kr // ret_dk, c_vr // ret_dv, c_gr // ret_dv),
                                    s_aux[0:1], dec_m, qd_m, kd_m[None], sd_m[None],
                                    b, seq // BLOCK, BLOCK, n_main, dims, "ret_prompt", False, False)
            o_r_aux = pad_aux([o_r_aux[0, :n_meta], o_r_aux[1:, :n_new].reshape(db * n_new, rv_w)], rv_w)

            w_out = att_w_out[a].astype(BF16)
            x = _mm_res_call([o_a, o_r], [o_a_aux, o_r_aux], [w_out[:sb_w], w_out[sb_w:]],
                             jnp.zeros((d,), F32), x, "w_out")

            def with_meta(main_cols, meta_rows):
                m = jnp.broadcast_to(meta_rows.reshape(1, n_meta, sb_heads, sb_hd), (b, n_meta, sb_heads, sb_hd))
                return jnp.concatenate([m, main_cols.reshape(b, seq, sb_heads, sb_hd)], axis=1)

            k_p.append(with_meta(qkv[:n_main, c_k:c_k + sb_w], meta_k))
            v_p.append(with_meta(qkv[:n_main, c_v:c_v + sb_w], meta_v))
            ret_p.append(s_main)
            k_s.append(samp_k.reshape(db, n_new, sb_heads, sb_hd))
            v_s.append(samp_v.reshape(db, n_new, sb_heads, sb_hd))
            ret_s.append(s_aux[1:])
        else:
            c = layer // 2
            u = _pw1_call(x, norm_mix[layer], conv_w_pw1[c].astype(BF16), conv_b_pw1[c])
            u_meta = u[r_meta:r_meta + n_meta]
            u_samp = u[r_samp:r_samp + db * n_new].reshape(db, n_new, conv_w)
            first = jnp.concatenate([jnp.zeros((CONV_HALO - n_meta, conv_w), F32), u_meta], axis=0)
            w_dw = conv_w_dw[c].astype(BF16).astype(F32)
            cm = _conv_main_call(u, first, w_dw, conv_b_dw[c], conv_ln_g[c], conv_ln_b[c], dims)
            cur_aux = jnp.concatenate(
                [u_meta[None], jnp.concatenate([u_samp, jnp.zeros((db, n_meta - n_new, conv_w), F32)], axis=1)],
                axis=0)
            hist = state_conv[c].astype(F32)
            prev_aux = jnp.concatenate(
                [jnp.zeros((1, CONV_HALO, conv_w), F32),
                 jnp.concatenate([jnp.zeros((db, CONV_HALO - hist.shape[1], conv_w), F32), hist], axis=1)], axis=0)
            ca = _conv_aux_call(cur_aux, prev_aux, w_dw, conv_b_dw[c], conv_ln_g[c], conv_ln_b[c])
            cm_aux = pad_aux([ca[0], ca[1:, :n_new].reshape(db * n_new, conv_w)], conv_w)
            x = _mm_res_call([cm], [cm_aux], [conv_w_pw2[c].astype(BF16)], conv_b_pw2[c], x, "pw2")
            keep = hist.shape[1]
            conv_p.append(u[:n_main].reshape(b, seq, conv_w)[:, seq - keep:])
            conv_s.append(jnp.concatenate([hist, u_samp], axis=1)[:, n_new:].astype(state_conv.dtype))
        x = _moe_layer(x, norm_ffn[layer], w_router[layer], moe_wg, moe_wu, moe_wd, layer, n_real,
                       n_groups, per_group)

    return (x[:n_main].reshape(b, seq, d), x[r_samp:r_samp + db * n_new].reshape(db, n_new, d),
            jnp.stack(k_p), jnp.stack(v_p), jnp.stack(ret_p), jnp.stack(conv_p),
            jnp.stack(k_s), jnp.stack(v_s), jnp.stack(ret_s), jnp.stack(conv_s))
```

```python
import functools

import numpy as np
import jax
import jax.numpy as jnp
from jax import lax
from jax.experimental import pallas as pl
from jax.experimental.pallas import tpu as pltpu

F32 = jnp.float32
BF16 = jnp.bfloat16

EPS = 1e-6
BLOCK = 128
ROPE_BASE = 10000.0
TOP_K = 2
LANES = 128
SUBLANES = 8
TM = 512
TN = 512
TN_WIDE = 1024
MOE_HC = 256
ATT_HPS = 4
ATT_SPS = 8
RET_HPS = 2
MOE_TM = 256
DMA_UNROLL = 8
CONV_TM = 128
CONV_HALO = 32
CONV_LANES = 256
VMEM_LIMIT = 56 * 1024 * 1024

_NT_DIMS = (((1,), (1,)), ((), ()))


def _params(sem, vmem=VMEM_LIMIT):
    return pltpu.CompilerParams(dimension_semantics=sem, vmem_limit_bytes=vmem)


def _store_normed(x_ref, g_ref, xn_ref):
    x = x_ref[...]
    ms = jnp.mean(x * x, axis=-1, keepdims=True)
    xn_ref[...] = (x * lax.rsqrt(ms + EPS) * g_ref[...]).astype(xn_ref.dtype)


def _win_kernel(x_ref, g_ref, w_ref, qg_ref, kg_ref, cos_ref, sin_ref, o_ref, xn_ref, *,
                t_q, t_k, t_rq, t_rk, t_rend, ret_scale):
    j = pl.program_id(1)

    @pl.when(j == 0)
    def _():
        _store_normed(x_ref, g_ref, xn_ref)

    acc = jnp.dot(xn_ref[...], w_ref[...], preferred_element_type=F32)
    tn = acc.shape[1]
    is_qk = j < t_k
    is_rot = (j >= t_rq) & (j < t_rend)

    @pl.when(is_qk)
    def _():
        gain = jnp.where(j < t_q, qg_ref[...], kg_ref[...])
        for g in range(tn // LANES):
            a = acc[:, g * LANES:(g + 1) * LANES]
            ms = jnp.mean(a * a, axis=-1, keepdims=True)
            o_ref[:, g * LANES:(g + 1) * LANES] = a * lax.rsqrt(ms + EPS) * gain

    @pl.when(is_rot)
    def _():
        scale = jnp.where(j < t_rk, ret_scale, 1.0).astype(F32)
        cos = cos_ref[...]
        sin = sin_ref[...]
        half = cos.shape[1]
        for hd in range(tn // (2 * half)):
            lo = hd * 2 * half
            x1 = acc[:, lo:lo + half]
            x2 = acc[:, lo + half:lo + 2 * half]
            o_ref[:, lo:lo + half] = (x1 * cos - x2 * sin) * scale
            o_ref[:, lo + half:lo + 2 * half] = (x1 * sin + x2 * cos) * scale

    @pl.when(jnp.logical_not(is_qk | is_rot))
    def _():
        o_ref[...] = acc


def _w_in_call(x, g, w_bf, q_gain, k_gain, cos_t, sin_t, dims):
    nt, d = x.shape
    n = w_bf.shape[1]
    sb_w, rqk_w = dims["sb_width"], dims["ret_qk_width"]
    half = dims["ret_qk_dim"] // 2
    tn = min(TN_WIDE, sb_w, rqk_w)
    t_q = sb_w // tn
    t_k = 2 * sb_w // tn
    t_rq = 3 * sb_w // tn
    t_rk = t_rq + rqk_w // tn
    t_rend = t_rk + rqk_w // tn
    kern = functools.partial(_win_kernel, t_q=t_q, t_k=t_k, t_rq=t_rq, t_rk=t_rk, t_rend=t_rend,
                             ret_scale=float(dims["ret_qk_dim"]) ** -0.5)
    return pl.pallas_call(
        kern,
        out_shape=jax.ShapeDtypeStruct((nt, n), F32),
        grid=(nt // TM, n // tn),
        in_specs=[
            pl.BlockSpec((TM, d), lambda i, j: (i, 0)),
            pl.BlockSpec((1, d), lambda i, j: (0, 0)),
            pl.BlockSpec((d, tn), lambda i, j: (0, j)),
            pl.BlockSpec((1, LANES), lambda i, j: (0, 0)),
            pl.BlockSpec((1, LANES), lambda i, j: (0, 0)),
            pl.BlockSpec((TM, half), lambda i, j: (i, 0)),
            pl.BlockSpec((TM, half), lambda i, j: (i, 0)),
        ],
        out_specs=pl.BlockSpec((TM, tn), lambda i, j: (i, j)),
        scratch_shapes=[pltpu.VMEM((TM, d), BF16)],
        compiler_params=_params(("parallel", "arbitrary")),
        name="w_in",
    )(x, g.reshape(1, d), w_bf, q_gain.reshape(1, -1), k_gain.reshape(1, -1), cos_t, sin_t)


def _pw1_kernel(x_ref, g_ref, wv_ref, wg_ref, bv_ref, bg_ref, o_ref, xn_ref):
    @pl.when(pl.program_id(1) == 0)
    def _():
        _store_normed(x_ref, g_ref, xn_ref)

    xn = xn_ref[...]
    val = jnp.dot(xn, wv_ref[...], preferred_element_type=F32) + bv_ref[...]
    gate = jnp.dot(xn, wg_ref[...], preferred_element_type=F32) + bg_ref[...]
    o_ref[...] = val * jax.nn.sigmoid(gate)


def _pw1_call(x, g, w_bf, b):
    nt, d = x.shape
    c = w_bf.shape[1] // 2
    tn = min(TN_WIDE, c)
    nj = c // tn
    b2 = b.reshape(1, 2 * c)
    return pl.pallas_call(
        _pw1_kernel,
        out_shape=jax.ShapeDtypeStruct((nt, c), F32),
        grid=(nt // TM, nj),
        in_specs=[
            pl.BlockSpec((TM, d), lambda i, j: (i, 0)),
            pl.BlockSpec((1, d), lambda i, j: (0, 0)),
            pl.BlockSpec((d, tn), lambda i, j: (0, j)),
            pl.BlockSpec((d, tn), lambda i, j: (0, j + nj)),
            pl.BlockSpec((1, tn), lambda i, j: (0, j)),
            pl.BlockSpec((1, tn), lambda i, j: (0, j + nj)),
        ],
        out_specs=pl.BlockSpec((TM, tn), lambda i, j: (i, j)),
        scratch_shapes=[pltpu.VMEM((TM, d), BF16)],
        compiler_params=_params(("parallel", "arbitrary")),
        name="pw1_glu",
    )(x, g.reshape(1, d), w_bf, w_bf, b2, b2)


def _mm_res_kernel(*refs, n_lhs):
    main_refs = refs[:n_lhs]
    aux_refs = refs[n_lhs:2 * n_lhs]
    w_refs = refs[2 * n_lhs:3 * n_lhs]
    b_ref, r_ref, o_ref = refs[3 * n_lhs:3 * n_lhs + 3]
    abf_refs = refs[3 * n_lhs + 3:]
    i = pl.program_id(0)
    is_aux = i == pl.num_programs(0) - 1

    @pl.when((pl.program_id(1) == 0) & jnp.logical_not(is_aux))
    def _():
        for a_ref, abf_ref in zip(main_refs, abf_refs):
            abf_ref[...] = a_ref[...].astype(BF16)

    @pl.when((pl.program_id(1) == 0) & is_aux)
    def _():
        for a_ref, abf_ref in zip(aux_refs, abf_refs):
            abf_ref[...] = a_ref[...].astype(BF16)

    acc = jnp.dot(abf_refs[0][...], w_refs[0][...], preferred_element_type=F32)
    for abf_ref, w_ref in zip(abf_refs[1:], w_refs[1:]):
        acc = acc + jnp.dot(abf_ref[...], w_ref[...], preferred_element_type=F32)
    o_ref[...] = r_ref[...] + (acc + b_ref[...])


def _mm_res_call(lhs_list, aux_list, w_list, bias, res, name):
    nt, d = res.shape
    n_lhs = len(lhs_list)
    n_main_tiles = nt // TM - 1
    in_specs = []
    for a in lhs_list:
        in_specs.append(pl.BlockSpec((TM, a.shape[1]), lambda i, j: (jnp.minimum(i, n_main_tiles - 1), 0)))
    for a in aux_list:
        in_specs.append(pl.BlockSpec((TM, a.shape[1]), lambda i, j: (0, 0)))
    for w in w_list:
        in_specs.append(pl.BlockSpec((w.shape[0], TN), lambda i, j: (0, j)))
    in_specs.append(pl.BlockSpec((1, TN), lambda i, j: (0, j)))
    in_specs.append(pl.BlockSpec((TM, TN), lambda i, j: (i, j)))
    return pl.pallas_call(
        functools.partial(_mm_res_kernel, n_lhs=n_lhs),
        out_shape=jax.ShapeDtypeStruct((nt, d), F32),
        grid=(nt // TM, d // TN),
        in_specs=in_specs,
        out_specs=pl.BlockSpec((TM, TN), lambda i, j: (i, j)),
        scratch_shapes=[pltpu.VMEM((TM, a.shape[1]), BF16) for a in lhs_list],
        compiler_params=_params(("parallel", "arbitrary")),
        name=name,
    )(*lhs_list, *aux_list, *w_list, bias.reshape(1, d), res)


def _router_kernel(x_ref, g_ref, w_ref, hn_ref, ids_ref, gates_ref, *, n_groups, per_group):
    x = x_ref[...]
    ms = jnp.mean(x * x, axis=-1, keepdims=True)
    hn = x * lax.rsqrt(ms + EPS) * g_ref[...]
    hn_ref[...] = hn
    lg = jnp.dot(hn.astype(BF16), w_ref[...].astype(BF16), preferred_element_type=F32)
    lane = lax.broadcasted_iota(jnp.int32, lg.shape, 1).astype(F32)
    neg = -jnp.inf
    far = float(LANES)

    def first_argmax(v, vmax):
        return jnp.min(jnp.where(v == vmax, lane, far), axis=-1, keepdims=True)

    gl = jnp.where(lane < n_groups, lg, neg)
    p = jnp.exp(gl - jnp.max(gl, axis=-1, keepdims=True))
    prob = p / jnp.sum(p, axis=-1, keepdims=True)
    g_val = jnp.max(prob, axis=-1, keepdims=True)
    g_idx = first_argmax(prob, g_val)
    lo = n_groups + g_idx * per_group
    el = jnp.where((lane >= lo) & (lane < lo + per_group), lg, neg)
    e1 = jnp.max(el, axis=-1, keepdims=True)
    i1 = first_argmax(el, e1)
    el2 = jnp.where(lane == i1, neg, el)
    e2 = jnp.max(el2, axis=-1, keepdims=True)
    i2 = first_argmax(el2, e2)
    t = jnp.exp(e2 - e1)
    den = 1.0 + t
    ids_ref[...] = jnp.where(lane == 0, i1 - n_groups, jnp.where(lane == 1, i2 - n_groups, 0.0)).astype(jnp.int32)
    gates_ref[...] = jnp.where(lane == 0, g_val * (1.0 / den), jnp.where(lane == 1, g_val * (t / den), 0.0))


def _router_call(x, g, w_router, n_groups, per_group):
    nt, d = x.shape
    tm = TM // 2
    return pl.pallas_call(
        functools.partial(_router_kernel, n_groups=n_groups, per_group=per_group),
        out_shape=(jax.ShapeDtypeStruct((nt, d), F32), jax.ShapeDtypeStruct((nt, LANES), jnp.int32),
                   jax.ShapeDtypeStruct((nt, LANES), F32)),
        grid=(nt // tm,),
        in_specs=[
            pl.BlockSpec((tm, d), lambda i: (i, 0)),
            pl.BlockSpec((1, d), lambda i: (0, 0)),
            pl.BlockSpec((d, LANES), lambda i: (0, 0)),
        ],
        out_specs=(pl.BlockSpec((tm, d), lambda i: (i, 0)), pl.BlockSpec((tm, LANES), lambda i: (i, 0)),
                   pl.BlockSpec((tm, LANES), lambda i: (i, 0))),
        compiler_params=_params(("parallel",)),
        name="router",
    )(x, g.reshape(1, d), w_router)


def _sb_weights_many(zs, mask, carries, tri):
    sps = [jnp.maximum(z, 0.0) + jnp.log1p(jnp.exp(-jnp.abs(z))) for z in zs]
    log_nots = [-sp for sp in sps]
    if mask is not None:
        log_nots = [jnp.where(mask, ln, 0.0) for ln in log_nots]
    his = [ln.astype(BF16) for ln in log_nots]
    r1s = [ln - hi.astype(F32) for ln, hi in zip(log_nots, his)]
    mids = [r1.astype(BF16) for r1 in r1s]
    los = [(r1 - mid.astype(F32)).astype(BF16) for r1, mid in zip(r1s, mids)]
    c_hi = [jnp.dot(p, tri, preferred_element_type=F32) for p in his]
    c_mid = [jnp.dot(p, tri, preferred_element_type=F32) for p in mids]
    c_lo = [jnp.dot(p, tri, preferred_element_type=F32) for p in los]
    ws = []
    for z, sp, a, b, c, carry in zip(zs, sps, c_hi, c_mid, c_lo, carries):
        w = jnp.exp((z - sp) + (((a + b) + c) + carry))
        ws.append(w if mask is None else jnp.where(mask, w, 0.0))
    new_carries = [carry + jnp.sum(ln, axis=-1, keepdims=True) for carry, ln in zip(carries, log_nots)]
    return ws, new_carries


def _attn_prompt_kernel(bias_ref, q_ref, k_ref, v_ref, mk_ref, mv_ref, tri_ref, o_ref, carry_ref, acc_ref, *,
                        n_meta, has_main, scale, hd):
    hg = pl.program_id(1)
    i = pl.program_id(2)
    tri = tri_ref[...]
    nq = q_ref.shape[0]
    n_h = q_ref.shape[1] // hd
    row = lax.broadcasted_iota(jnp.int32, (nq, BLOCK), 0)
    col = lax.broadcasted_iota(jnp.int32, (nq, BLOCK), 1)
    qs = [q_ref[:, j * hd:(j + 1) * hd].astype(BF16) for j in range(n_h)]
    biases = [bias_ref[hg * n_h + j] for j in range(n_h)]

    def step(kv_rows, kr, vr, mask):
        zs = [lax.dot_general(qs[j], kr[kv_rows, j * hd:(j + 1) * hd].astype(BF16), _NT_DIMS,
                              preferred_element_type=F32) * scale + biases[j] for j in range(n_h)]
        ws, carries = _sb_weights_many(zs, mask, [carry_ref[j] for j in range(n_h)], tri)
        pvs = [jnp.dot(ws[j].astype(BF16), vr[kv_rows, j * hd:(j + 1) * hd].astype(BF16),
                       preferred_element_type=F32) for j in range(n_h)]
        for j in range(n_h):
            carry_ref[j] = carries[j]
            acc_ref[j] += pvs[j]

    carry_ref[...] = jnp.zeros_like(carry_ref)
    acc_ref[...] = jnp.zeros_like(acc_ref)
    if has_main:
        step(pl.ds(pl.multiple_of(i * BLOCK, BLOCK), BLOCK), k_ref, v_ref, col < row)

        def body(n, _):
            step(pl.ds(pl.multiple_of((i - 1 - n) * BLOCK, BLOCK), BLOCK), k_ref, v_ref, None)
            return 0

        lax.fori_loop(0, i, body, 0)
        meta_mask = col < n_meta
    else:
        meta_mask = (col < n_meta) & (col < row)
    step(slice(None), mk_ref, mv_ref, meta_mask)
    for j in range(n_h):
        o_ref[:, j * hd:(j + 1) * hd] = acc_ref[j]


def _attn_prompt_call(qkv, sb_bias, tri, dims, has_main):
    b, seq, h, hd = dims["batch"], dims["seq"], dims["sb_heads"], dims["sb_head_dim"]
    nq = seq // BLOCK
    aux_blk = (b * seq) // BLOCK
    hw = ATT_HPS * hd
    n_hg = h // ATT_HPS
    kern = functools.partial(_attn_prompt_kernel, n_meta=dims["n_meta"], has_main=has_main,
                             scale=float(hd) ** -0.5, hd=hd)
    if has_main:
        grid = (b, n_hg, nq)
        q_map = lambda bb, hh, ii: (bb * nq + ii, hh)
        out_rows = b * seq
    else:
        grid = (1, n_hg, 1)
        q_map = lambda bb, hh, ii: (aux_blk, hh)
        out_rows = BLOCK
    o_map = q_map if has_main else (lambda bb, hh, ii: (0, hh))
    return pl.pallas_call(
        kern,
        out_shape=jax.ShapeDtypeStruct((out_rows, h * hd), F32),
        grid=grid,
        in_specs=[
            pl.BlockSpec(memory_space=pltpu.SMEM),
            pl.BlockSpec((BLOCK, hw), q_map),
            pl.BlockSpec((seq, hw), lambda bb, hh, ii: (bb, n_hg + hh)),
            pl.BlockSpec((seq, hw), lambda bb, hh, ii: (bb, 2 * n_hg + hh)),
            pl.BlockSpec((BLOCK, hw), lambda bb, hh, ii: (aux_blk, n_hg + hh)),
            pl.BlockSpec((BLOCK, hw), lambda bb, hh, ii: (aux_blk, 2 * n_hg + hh)),
            pl.BlockSpec((BLOCK, BLOCK), lambda bb, hh, ii: (0, 0)),
        ],
        out_specs=pl.BlockSpec((BLOCK, hw), o_map),
        scratch_shapes=[pltpu.VMEM((ATT_HPS, BLOCK, 1), F32), pltpu.VMEM((ATT_HPS, BLOCK, hd), F32)],
        compiler_params=_params(("parallel", "parallel", "arbitrary")),
        name="attn_prompt" if has_main else "attn_meta",
    )(sb_bias.astype(F32), qkv, qkv, qkv, qkv, qkv, tri)


def _attn_sample_kernel(pt_ref, q_ref, bias_ref, kn_ref, vn_ref, *rest, n_heads, n_new, n_seq, scale):
    kp_refs = rest[:n_seq]
    vp_refs = rest[n_seq:2 * n_seq]
    tri_ref, o_ref, carry_ref, acc_ref = rest[2 * n_seq:]
    j = pl.program_id(1)
    last = pl.num_programs(1) - 1
    tri = tri_ref[...]
    nrow = q_ref.shape[1]

    def process(kf_refs, vf_refs, mask):
        zs = []
        for s in range(n_seq):
            q = q_ref[s]
            zh = []
            for hh in range(n_heads):
                k_h = kf_refs[s][pl.ds(hh, BLOCK, stride=n_heads), :].astype(BF16)
                zh.append(lax.dot_general(q[hh * n_new:(hh + 1) * n_new].astype(BF16), k_h, _NT_DIMS,
                                          preferred_element_type=F32))
            zs.append(jnp.concatenate(zh, axis=0) * scale + bias_ref[...])
        ws, carries = _sb_weights_many(zs, mask, [carry_ref[s] for s in range(n_seq)], tri)
        for s in range(n_seq):
            outs = []
            for hh in range(n_heads):
                v_h = vf_refs[s][pl.ds(hh, BLOCK, stride=n_heads), :].astype(BF16)
                outs.append(jnp.dot(ws[s][hh * n_new:(hh + 1) * n_new].astype(BF16), v_h,
                                    preferred_element_type=F32))
            carry_ref[s] = carries[s]
            acc_ref[s] += jnp.concatenate(outs, axis=0)

    @pl.when(j == 0)
    def _():
        carry_ref[...] = jnp.zeros_like(carry_ref)
        acc_ref[...] = jnp.zeros_like(acc_ref)
        row = lax.broadcasted_iota(jnp.int32, (nrow, BLOCK), 0)
        col = lax.broadcasted_iota(jnp.int32, (nrow, BLOCK), 1)
        process([kn_ref.at[s] for s in range(n_seq)], [vn_ref.at[s] for s in range(n_seq)],
                col < (row % n_new))

    @pl.when(j > 0)
    def _():
        process(kp_refs, vp_refs, None)

    @pl.when(j == last)
    def _():
        o_ref[...] = acc_ref[...]


def _attn_sample_call(page_table, q_s, bias_col, k_new, v_new, cache_k4, cache_v4, tri, layer, dims):
    db, n_pages = page_table.shape
    h, hd, n_new = dims["sb_heads"], dims["sb_head_dim"], dims["dec_seq"]
    rows = cache_k4.shape[2]
    n_seq = ATT_SPS if db % ATT_SPS == 0 else 1
    kern = functools.partial(_attn_sample_kernel, n_heads=h, n_new=n_new, n_seq=n_seq, scale=float(hd) ** -0.5)

    def page_map(s):
        return lambda g, jj, pt: (layer, pt[(g * n_seq + s) * n_pages + n_pages - jnp.maximum(jj, 1)], 0, 0)

    page_specs = [pl.BlockSpec((None, None, rows, hd), page_map(s)) for s in range(n_seq)]
    gs = pltpu.PrefetchScalarGridSpec(
        num_scalar_prefetch=1,
        grid=(db // n_seq, n_pages + 1),
        in_specs=[
            pl.BlockSpec((n_seq, h * n_new, hd), lambda g, jj, pt: (g, 0, 0)),
            pl.BlockSpec((h * n_new, 1), lambda g, jj, pt: (0, 0)),
            pl.BlockSpec((n_seq, rows, hd), lambda g, jj, pt: (g, 0, 0)),
            pl.BlockSpec((n_seq, rows, hd), lambda g, jj, pt: (g, 0, 0)),
            *page_specs, *page_specs,
            pl.BlockSpec((BLOCK, BLOCK), lambda g, jj, pt: (0, 0)),
        ],
        out_specs=pl.BlockSpec((n_seq, h * n_new, hd), lambda g, jj, pt: (g, 0, 0)),
        scratch_shapes=[pltpu.VMEM((n_seq, h * n_new, 1), F32), pltpu.VMEM((n_seq, h * n_new, hd), F32)],
    )
    return pl.pallas_call(
        kern,
        out_shape=jax.ShapeDtypeStruct((db, h * n_new, hd), F32),
        grid_spec=gs,
        compiler_params=_params(("parallel", "arbitrary")),
        name="attn_sample",
    )(page_table.reshape(-1), q_s, bias_col, k_new, v_new, *([cache_k4] * n_seq), *([cache_v4] * n_seq), tri)


def _ret_kernel(q_ref, k_ref, v_ref, g_ref, s0_ref, dec_ref, qd_ref, kd_ref, sd_ref, o_ref, sout_ref, state_ref, *,
                dk, dv):
    c = pl.program_id(2)
    n_h = state_ref.shape[0]
    hs = range(n_h)

    @pl.when(c == 0)
    def _():
        state_ref[...] = s0_ref[...]

    ks = [k_ref[:, j * dk:(j + 1) * dk] for j in hs]
    qbs = [q_ref[:, j * dk:(j + 1) * dk].astype(BF16) for j in hs]
    vbs = [v_ref[:, j * dv:(j + 1) * dv].astype(BF16) for j in hs]
    states = [state_ref[j] for j in hs]
    scores = [lax.dot_general(qbs[j], ks[j].astype(BF16), _NT_DIMS, preferred_element_type=F32) * dec_ref[j]
              for j in hs]
    intra = [jnp.dot(scores[j].astype(BF16), vbs[j], preferred_element_type=F32) for j in hs]
    cross = [jnp.dot(qbs[j], states[j].astype(BF16), preferred_element_type=F32) * qd_ref[j] for j in hs]
    kds = [(ks[j] * kd_ref[j]).astype(BF16) for j in hs]
    upd = [lax.dot_general(kds[j], vbs[j], (((0,), (0,)), ((), ())), preferred_element_type=F32) for j in hs]
    for j in hs:
        state_ref[j] = sd_ref[j] * states[j] + upd[j]
        o = intra[j] + cross[j]
        mu = jnp.mean(o, axis=-1, keepdims=True)
        var = jnp.mean(jnp.square(o - mu), axis=-1, keepdims=True)
        o_ref[:, j * dv:(j + 1) * dv] = ((o - mu) * lax.rsqrt(var + EPS)) * jax.nn.silu(g_ref[:, j * dv:(j + 1) * dv])

    @pl.when(c == pl.num_programs(2) - 1)
    def _():
        sout_ref[...] = state_ref[...]


def _ret_call(q, k, v, g, col_off, s0, dec, qd, kd, sd, n_seq, n_chunks, rows, out_rows, dims, name,
              s0_per_seq, kd_per_seq):
    nh, dk, dv = dims["ret_heads"], dims["ret_qk_dim"], dims["ret_v_dim"]
    n_h = RET_HPS
    qo, ko, vo, go = col_off

    def rmap(off, width):
        return lambda s, hg, c: (s * n_chunks + c, off // (n_h * width) + hg)

    s0_map = (lambda s, hg, c: (s, hg, 0, 0)) if s0_per_seq else (lambda s, hg, c: (0, hg, 0, 0))
    kd_map = (lambda s, hg, c: (s, hg, 0, 0)) if kd_per_seq else (lambda s, hg, c: (0, hg, 0, 0))
    return pl.pallas_call(
        functools.partial(_ret_kernel, dk=dk, dv=dv),
        out_shape=(jax.ShapeDtypeStruct((out_rows, nh * dv), F32),
                   jax.ShapeDtypeStruct((n_seq, nh, dk, dv), F32)),
        grid=(n_seq, nh // n_h, n_chunks),
        in_specs=[
            pl.BlockSpec((rows, n_h * dk), rmap(qo, dk)),
            pl.BlockSpec((rows, n_h * dk), rmap(ko, dk)),
            pl.BlockSpec((rows, n_h * dv), rmap(vo, dv)),
            pl.BlockSpec((rows, n_h * dv), rmap(go, dv)),
            pl.BlockSpec((None, n_h, dk, dv), s0_map),
            pl.BlockSpec((n_h, rows, rows), lambda s, hg, c: (hg, 0, 0)),
            pl.BlockSpec((n_h, rows, 1), lambda s, hg, c: (hg, 0, 0)),
            pl.BlockSpec((None, n_h, rows, 1), kd_map),
            pl.BlockSpec((None, n_h, 1, 1), kd_map),
        ],
        out_specs=(pl.BlockSpec((rows, n_h * dv), rmap(0, dv)),
                   pl.BlockSpec((None, n_h, dk, dv), lambda s, hg, c: (s, hg, 0, 0))),
        scratch_shapes=[pltpu.VMEM((n_h, dk, dv), F32)],
        compiler_params=_params(("parallel", "parallel", "arbitrary")),
        name=name,
    )(q, k, v, g, s0, dec, qd, kd, sd)


def _conv_kernel(prev_ref, first_ref, cur_ref, w_ref, b_ref, lg_ref, lb_ref, o_ref, ubuf_ref, qbuf_ref, *,
                 tiles_per_seq, width):
    tm, ch = cur_ref.shape
    if tiles_per_seq is None:
        halo = first_ref[...]
    else:
        is_first = (pl.program_id(0) % tiles_per_seq) == 0
        halo = jnp.where(is_first, first_ref[...], prev_ref[...])
    ubuf_ref[0:CONV_HALO, :] = halo.astype(BF16).astype(F32)
    ubuf_ref[CONV_HALO:CONV_HALO + tm, :] = cur_ref[...].astype(BF16).astype(F32)
    base = CONV_HALO - (width - 1)

    def chunk(cc, total):
        lo = pl.multiple_of(cc * CONV_LANES, CONV_LANES)
        lanes = pl.ds(lo, CONV_LANES)
        acc = None
        for rem in range(SUBLANES):
            taps = [kk for kk in range(width) if (base + kk) % SUBLANES == rem]
            if not taps:
                continue
            rows = tm if rem == 0 else tm + SUBLANES
            part = None
            for kk in taps:
                start = base + kk - rem
                term = w_ref[kk:kk + 1, lanes] * ubuf_ref[start:start + rows, lanes]
                part = term if part is None else part + term
            if rem:
                qbuf_ref[rem - 1, 0:rows, :] = part
                part = qbuf_ref[rem - 1, rem:rem + tm, :]
            acc = part if acc is None else acc + part
        acc = acc + b_ref[:, lanes]
        o_ref[:, pl.ds(lo, CONV_LANES)] = acc
        return total + jnp.sum(acc, axis=-1, keepdims=True)

    total = lax.fori_loop(0, ch // CONV_LANES, chunk, jnp.zeros((tm, 1), F32))
    mu = total / ch
    y = o_ref[...]
    var = jnp.mean(jnp.square(y - mu), axis=-1, keepdims=True)
    yn = (y - mu) * lax.rsqrt(var + EPS) * lg_ref[...] + lb_ref[...]
    o_ref[...] = jax.nn.silu(yn)


def _conv_main_call(u, first, w_dw, b_dw, ln_g, ln_b, dims):
    nt, ch = u.shape
    b, seq = dims["batch"], dims["seq"]
    tiles_per_seq = seq // CONV_TM
    width = w_dw.shape[0]
    ratio = CONV_TM // CONV_HALO
    kern = functools.partial(_conv_kernel, tiles_per_seq=tiles_per_seq, width=width)
    return pl.pallas_call(
        kern,
        out_shape=jax.ShapeDtypeStruct((b * seq, ch), F32),
        grid=(b * tiles_per_seq,),
        in_specs=[
            pl.BlockSpec((CONV_HALO, ch), lambda i: (jnp.maximum(i * ratio - 1, 0), 0)),
            pl.BlockSpec((CONV_HALO, ch), lambda i: (0, 0)),
            pl.BlockSpec((CONV_TM, ch), lambda i: (i, 0)),
            pl.BlockSpec((width, ch), lambda i: (0, 0)),
            pl.BlockSpec((1, ch), lambda i: (0, 0)),
            pl.BlockSpec((1, ch), lambda i: (0, 0)),
            pl.BlockSpec((1, ch), lambda i: (0, 0)),
        ],
        out_specs=pl.BlockSpec((CONV_TM, ch), lambda i: (i, 0)),
        scratch_shapes=[pltpu.VMEM((CONV_HALO + CONV_TM, ch), F32),
                        pltpu.VMEM((SUBLANES - 1, CONV_TM + SUBLANES, CONV_LANES), F32)],
        compiler_params=_params(("parallel",)),
        name="conv_main",
    )(u, first, u, w_dw, b_dw.reshape(1, ch), ln_g.reshape(1, ch), ln_b.reshape(1, ch))


def _conv_aux_call(cur, prev, w_dw, b_dw, ln_g, ln_b):
    n_seq, rows, ch = cur.shape
    width = w_dw.shape[0]
    kern = functools.partial(_conv_kernel, tiles_per_seq=None, width=width)
    return pl.pallas_call(
        kern,
        out_shape=jax.ShapeDtypeStruct((n_seq, rows, ch), F32),
        grid=(n_seq,),
        in_specs=[
            pl.BlockSpec((None, CONV_HALO, ch), lambda i: (i, 0, 0)),
            pl.BlockSpec((None, CONV_HALO, ch), lambda i: (i, 0, 0)),
            pl.BlockSpec((None, rows, ch), lambda i: (i, 0, 0)),
            pl.BlockSpec((width, ch), lambda i: (0, 0)),
            pl.BlockSpec((1, ch), lambda i: (0, 0)),
            pl.BlockSpec((1, ch), lambda i: (0, 0)),
            pl.BlockSpec((1, ch), lambda i: (0, 0)),
        ],
        out_specs=pl.BlockSpec((None, rows, ch), lambda i: (i, 0, 0)),
        scratch_shapes=[pltpu.VMEM((CONV_HALO + rows, ch), F32),
                        pltpu.VMEM((SUBLANES - 1, rows + SUBLANES, CONV_LANES), F32)],
        compiler_params=_params(("parallel",)),
        name="conv_aux",
    )(prev, prev, cur, w_dw, b_dw.reshape(1, ch), ln_g.reshape(1, ch), ln_b.reshape(1, ch))


def _moe_kernel(te_ref, tok_ref, dst_ref, nv_ref, nu_ref, hn_hbm, wg_ref, wu_ref, wd_ref, gw_ref, y_hbm,
                xbuf, ybuf, gsem, ssem, zsem, *, n_real, n_rows):
    t = pl.program_id(0)
    n_used = nu_ref[0]
    tm = ybuf.shape[0]
    slot = t % 2

    def gather_copy(tile, buf_slot, r):
        tok = tok_ref[tile * tm + r]
        return pltpu.make_async_copy(hn_hbm.at[pl.ds(tok, 1)], xbuf.at[buf_slot, pl.ds(r, 1)], gsem.at[buf_slot])

    def for_rows(n, fn):
        def group(gi, _):
            base = pl.multiple_of(gi * DMA_UNROLL, DMA_UNROLL)
            for u in range(DMA_UNROLL):
                fn(base + u)
            return 0
        lax.fori_loop(0, n // DMA_UNROLL, group, 0)

        def rest(r, _):
            fn(r)
            return 0
        lax.fori_loop((n // DMA_UNROLL) * DMA_UNROLL, n, rest, 0)

    def start_gather(tile, buf_slot):
        for_rows(tm, lambda r: gather_copy(tile, buf_slot, r).start())

    def wait_gather(buf_slot):
        pltpu.make_async_copy(hn_hbm.at[pl.ds(0, tm)], xbuf.at[buf_slot], gsem.at[buf_slot]).wait()

    def scatter_copy(tile, r):
        return pltpu.make_async_copy(ybuf.at[pl.ds(r, 1)], y_hbm.at[pl.ds(dst_ref[tile * tm + r], 1)], ssem)

    def start_scatter(tile):
        for_rows(nv_ref[tile], lambda r: scatter_copy(tile, r).start())

    def wait_scatter(tile):
        n = nv_ref[tile]

        def group(gi, _):
            pltpu.make_async_copy(ybuf.at[pl.ds(0, DMA_UNROLL)], y_hbm.at[pl.ds(0, DMA_UNROLL)], ssem).wait()
            return 0
        lax.fori_loop(0, n // DMA_UNROLL, group, 0)

        def rest(r, _):
            scatter_copy(tile, r).wait()
            return 0
        lax.fori_loop((n // DMA_UNROLL) * DMA_UNROLL, n, rest, 0)

    @pl.when(t == 0)
    def _():
        ybuf[...] = jnp.zeros_like(ybuf)
        n_pad = n_rows - n_real
        copies = []
        for s in range(TOP_K):
            done = 0
            while done < n_pad:
                cnt = min(tm, n_pad - done)
                copies.append(pltpu.make_async_copy(ybuf.at[pl.ds(0, cnt)],
                                                    y_hbm.at[pl.ds(s * n_rows + n_real + done, cnt)], zsem))
                done += cnt
        for cp in copies:
            cp.start()
        for cp in copies:
            cp.wait()
        start_gather(0, 0)

    @pl.when(t < n_used)
    def _():
        wait_gather(slot)

        @pl.when(t + 1 < n_used)
        def _():
            start_gather(t + 1, 1 - slot)

        x = xbuf[slot].astype(BF16)
        n_hid = wg_ref.shape[1]
        hc = min(MOE_HC, n_hid)
        y = None
        for c in range(n_hid // hc):
            gate = jnp.dot(x, wg_ref[:, c * hc:(c + 1) * hc].astype(BF16), preferred_element_type=F32)
            up = jnp.dot(x, wu_ref[:, c * hc:(c + 1) * hc].astype(BF16), preferred_element_type=F32)
            hid = (jax.nn.silu(gate) * up).astype(BF16)
            part = jnp.dot(hid, wd_ref[c * hc:(c + 1) * hc, :].astype(BF16), preferred_element_type=F32)
            y = part if y is None else y + part
        y = y * gw_ref[...]

        @pl.when(t > 0)
        def _():
            wait_scatter(t - 1)

        ybuf[...] = y
        start_scatter(t)

        @pl.when(t == n_used - 1)
        def _():
            wait_scatter(t)


def _moe_call(hn, tile_expert, tok_pad, dst_pad, n_valid, n_used, gate_w, w_gate, w_up, w_down, layer, n_real):
    nt, d = hn.shape
    cap = tok_pad.shape[0]
    n_tiles = cap // MOE_TM
    hid = w_gate.shape[-1]
    kern = functools.partial(_moe_kernel, n_real=n_real, n_rows=nt)
    gs = pltpu.PrefetchScalarGridSpec(
        num_scalar_prefetch=5,
        grid=(n_tiles,),
        in_specs=[
            pl.BlockSpec(memory_space=pl.ANY),
            pl.BlockSpec((None, None, d, hid), lambda t, te, *_: (layer, te[t], 0, 0)),
            pl.BlockSpec((None, None, d, hid), lambda t, te, *_: (layer, te[t], 0, 0)),
            pl.BlockSpec((None, None, hid, d), lambda t, te, *_: (layer, te[t], 0, 0)),
            pl.BlockSpec((MOE_TM, 1), lambda t, *_: (t, 0)),
        ],
        out_specs=pl.BlockSpec(memory_space=pl.ANY),
        scratch_shapes=[
            pltpu.VMEM((2, MOE_TM, d), F32),
            pltpu.VMEM((MOE_TM, d), F32),
            pltpu.SemaphoreType.DMA((2,)),
            pltpu.SemaphoreType.DMA(()),
            pltpu.SemaphoreType.DMA(()),
        ],
    )
    return pl.pallas_call(
        kern,
        out_shape=jax.ShapeDtypeStruct((TOP_K * nt, d), F32),
        grid_spec=gs,
        compiler_params=_params(("arbitrary",)),
        name="moe_experts",
    )(tile_expert, tok_pad, dst_pad, n_valid, n_used, hn, w_gate, w_up, w_down, gate_w)


def _combine_kernel(x_ref, y0_ref, y1_ref, o_ref):
    o_ref[...] = x_ref[...] + (y0_ref[...] + y1_ref[...])


def _combine_call(x, y2):
    nt, d = x.shape
    nb = nt // TM
    return pl.pallas_call(
        _combine_kernel,
        out_shape=jax.ShapeDtypeStruct((nt, d), F32),
        grid=(nb,),
        in_specs=[
            pl.BlockSpec((TM, d), lambda i: (i, 0)),
            pl.BlockSpec((TM, d), lambda i: (i, 0)),
            pl.BlockSpec((TM, d), lambda i: (i + nb, 0)),
        ],
        out_specs=pl.BlockSpec((TM, d), lambda i: (i, 0)),
        compiler_params=_params(("parallel",)),
        name="moe_combine",
    )(x, y2, y2)


def _route(ids, gates, n_real, n_exp, nt):
    n_assign = n_real * TOP_K
    blk = MOE_TM
    n_fill = n_exp * (blk - 1)
    cap = (n_assign + n_fill + blk - 1) // blk * blk
    e_flat = ids[:n_real, :TOP_K].reshape(n_assign)
    w_flat = gates[:n_real, :TOP_K].reshape(n_assign)
    t_flat = jnp.repeat(jnp.arange(n_real, dtype=jnp.int32), TOP_K)
    s_flat = jnp.tile(jnp.arange(TOP_K, dtype=jnp.int32), n_real)
    counts = jnp.sum((e_flat[:, None] == jnp.arange(n_exp, dtype=jnp.int32)[None, :]).astype(jnp.int32), axis=0)
    padded = (counts + blk - 1) // blk * blk
    fill_key = jnp.where(jnp.arange(blk - 1, dtype=jnp.int32)[None, :] < (padded - counts)[:, None],
                         jnp.arange(n_exp, dtype=jnp.int32)[:, None], n_exp).reshape(n_fill)
    n_tail = cap - n_assign - n_fill
    keys = jnp.concatenate([e_flat, fill_key, jnp.full((n_tail,), n_exp, jnp.int32)])
    n_extra = n_fill + n_tail
    tok = jnp.concatenate([t_flat, jnp.full((n_extra,), n_real, jnp.int32)])
    dst = jnp.concatenate([s_flat * nt + t_flat, jnp.full((n_extra,), -1, jnp.int32)])
    wgt = jnp.concatenate([w_flat, jnp.zeros((n_extra,), F32)])
    keys_s, tok_pad, dst_pad, w_pad = lax.sort((keys, tok, dst, wgt), num_keys=1, is_stable=True)
    n_tiles = cap // blk
    tile_expert = jnp.minimum(keys_s[::blk], n_exp - 1)
    n_valid = jnp.sum((dst_pad >= 0).astype(jnp.int32).reshape(n_tiles, blk), axis=1)
    n_used = (jnp.sum(padded) // blk).astype(jnp.int32).reshape(1)
    return tile_expert, tok_pad, dst_pad, n_valid, n_used, w_pad.reshape(cap, 1)


def _moe_layer(x, g, w_router, w_gate, w_up, w_down, layer, n_real, n_groups, per_group):
    hn, ids, gates = _router_call(x, g, w_router, n_groups, per_group)
    tile_expert, tok_pad, dst_pad, n_valid, n_used, gate_w = _route(ids, gates, n_real, n_groups * per_group,
                                                                   x.shape[0])
    y2 = _moe_call(hn, tile_expert, tok_pad, dst_pad, n_valid, n_used, gate_w, w_gate, w_up, w_down, layer, n_real)
    return _combine_call(x, y2)


def _decay_tables(n_heads, rows, live_rows):
    log_gamma = jnp.log1p(-jnp.exp2(-5.0 - jnp.arange(n_heads, dtype=F32)))
    idx = jnp.arange(rows, dtype=F32)
    rel = idx[:, None] - idx[None, :]
    dec = jnp.where(rel >= 0, jnp.exp(jnp.maximum(rel, 0.0)[None] * log_gamma[:, None, None]), 0.0)
    qd = jnp.exp((idx[None, :] + 1.0) * log_gamma[:, None])[:, :, None]
    kd = jnp.exp((live_rows - 1.0 - idx)[None, :] * log_gamma[:, None])
    kd = jnp.where(idx[None, :] < live_rows, kd, 0.0)[:, :, None]
    sd = jnp.exp(live_rows * log_gamma)[:, None, None]
    return dec, qd, kd, sd


def kernel(x_prompt, x_sample, cache_k, cache_v, state_ret, state_conv, page_table, meta_tokens, norm_mix, norm_ffn, att_w_in, att_q_gain, att_k_gain, att_sb_bias, att_w_out, conv_w_pw1, conv_b_pw1, conv_w_dw, conv_b_dw, conv_ln_g, conv_ln_b, conv_w_pw2, conv_b_pw2, moe_router_group, moe_router_expert, moe_w_gate, moe_w_up, moe_w_down):
    b, seq, d = x_prompt.shape
    db, n_new, _ = x_sample.shape
    n_meta = meta_tokens.shape[0]
    depth = norm_mix.shape[0]
    _, n_pool, page, sb_heads, sb_hd = cache_k.shape
    _, _, ret_heads, ret_dk, ret_dv = state_ret.shape
    n_groups, per_group = moe_router_expert.shape[2:]
    past_len = page_table.shape[1] * page
    sb_w = sb_heads * sb_hd
    rqk_w = ret_heads * ret_dk
    rv_w = ret_heads * ret_dv
    conv_w = conv_w_dw.shape[2]
    dims = dict(batch=b, seq=seq, n_meta=n_meta, sb_heads=sb_heads, sb_head_dim=sb_hd, sb_width=sb_w,
                ret_heads=ret_heads, ret_qk_dim=ret_dk, ret_v_dim=ret_dv, ret_qk_width=rqk_w,
                dec_seq=n_new)
    assert seq % TM == 0 and (b * seq) % TM == 0 and n_meta + db * n_new <= BLOCK
    assert page == BLOCK and sb_hd == LANES and ret_dk == 2 * LANES and n_meta % 8 == 0 and n_new == 8

    n_main = b * seq
    nt = n_main + TM
    r_meta = n_main
    r_samp = n_main + n_meta
    n_real = r_samp + db * n_new
    aux_pad = nt - n_real

    x = jnp.concatenate([x_prompt.reshape(n_main, d), meta_tokens.astype(F32), x_sample.reshape(db * n_new, d),
                         jnp.zeros((aux_pad, d), F32)], axis=0)

    pos = jnp.concatenate([jnp.tile(n_meta + jnp.arange(seq), b), jnp.arange(n_meta),
                           jnp.tile(past_len + jnp.arange(n_new), db), jnp.zeros((aux_pad,), jnp.int32)])
    half = ret_dk // 2
    inv_freq = ROPE_BASE ** (-jnp.arange(half, dtype=F32) / half)
    ang = pos.astype(F32)[:, None] * inv_freq[None, :]
    cos_t, sin_t = jnp.cos(ang), jnp.sin(ang)

    idx = jnp.arange(BLOCK)
    tri = (idx[:, None] > idx[None, :]).astype(BF16)

    dec_m, qd_m, kd_m, sd_m = _decay_tables(ret_heads, BLOCK, float(BLOCK))
    aux_rows = n_meta
    dec_a, qd_a, kd_meta, sd_meta = _decay_tables(ret_heads, aux_rows, float(n_meta))
    _, _, kd_s, sd_s = _decay_tables(ret_heads, aux_rows, float(n_new))
    kd_aux = jnp.concatenate([kd_meta[None], jnp.broadcast_to(kd_s[None], (db,) + kd_s.shape)], axis=0)
    sd_aux = jnp.concatenate([sd_meta[None], jnp.broadcast_to(sd_s[None], (db,) + sd_s.shape)], axis=0)

    cache_k4 = cache_k.reshape(cache_k.shape[0], n_pool, page * sb_heads, sb_hd)
    cache_v4 = cache_v.reshape(cache_v.shape[0], n_pool, page * sb_heads, sb_hd)

    w_router = jnp.concatenate(
        [moe_router_group, moe_router_expert.reshape(depth, d, n_groups * per_group),
         jnp.zeros((depth, d, LANES - n_groups - n_groups * per_group), F32)], axis=-1)

    def pad_aux(rows_list, width):
        used = sum(r.shape[0] for r in rows_list)
        return jnp.concatenate(list(rows_list) + [jnp.zeros((TM - used, width), F32)], axis=0)

    k_p, v_p, ret_p, conv_p, k_s, v_s, ret_s, conv_s = [], [], [], [], [], [], [], []
    for layer in range(depth):
        if layer % 2 == 0:
            a = layer // 2
            qkv = _w_in_call(x, norm_mix[layer], att_w_in[a].astype(BF16), att_q_gain[a], att_k_gain[a],
                             cos_t, sin_t, dims)
            c_k, c_v = sb_w, 2 * sb_w
            c_qr = 3 * sb_w
            c_kr = c_qr + rqk_w
            c_vr = c_kr + rqk_w
            c_gr = c_vr + rv_w
            aux = qkv[n_main:n_main + BLOCK]
            meta_k, meta_v = aux[:n_meta, c_k:c_k + sb_w], aux[:n_meta, c_v:c_v + sb_w]
            samp = aux[n_meta:n_meta + db * n_new]
            samp_k, samp_v = samp[:, c_k:c_k + sb_w], samp[:, c_v:c_v + sb_w]

            o_a = _attn_prompt_call(qkv, att_sb_bias[a], tri, dims, True)
            o_a_meta = _attn_prompt_call(qkv, att_sb_bias[a], tri, dims, False)[:n_meta]
            q_s = samp[:, :sb_w].reshape(db, n_new, sb_heads, sb_hd).transpose(0, 2, 1, 3)
            q_s = q_s.reshape(db, sb_heads * n_new, sb_hd)

            def new_page(rows):
                pg = rows.reshape(db, n_new * sb_heads, sb_hd)
                return jnp.concatenate([pg, jnp.zeros((db, (page - n_new) * sb_heads, sb_hd), F32)], axis=1)

            bias_col = jnp.repeat(att_sb_bias[a].astype(F32), n_new).reshape(sb_heads * n_new, 1)
            o_a_s = _attn_sample_call(page_table, q_s, bias_col, new_page(samp_k), new_page(samp_v),
                                      cache_k4, cache_v4, tri, a, dims)
            o_a_s = o_a_s.reshape(db, sb_heads, n_new, sb_hd).transpose(0, 2, 1, 3).reshape(db * n_new, sb_w)
            o_a_aux = pad_aux([o_a_meta, o_a_s], sb_w)

            def aux_seqs(lo, width):
                m = aux[:n_meta, lo:lo + width]
                m = jnp.concatenate([m, jnp.zeros((aux_rows - n_meta, width), F32)], axis=0)[None]
                s = samp[:, lo:lo + width].reshape(db, n_new, width)
                s = jnp.concatenate([s, jnp.zeros((db, aux_rows - n_new, width), F32)], axis=1)
                return jnp.concatenate([m, s], axis=0).reshape((db + 1) * aux_rows, width)

            s0_aux = jnp.concatenate([jnp.zeros((1, ret_heads, ret_dk, ret_dv), F32), state_ret[a]], axis=0)
            o_r_aux, s_aux = _ret_call(aux_seqs(c_qr, rqk_w), aux_seqs(c_kr, rqk_w), aux_seqs(c_vr, rv_w),
                                       aux_seqs(c_gr, rv_w), (0, 0, 0, 0), s0_aux, dec_a, qd_a, kd_aux, sd_aux,
                                       db + 1, 1, aux_rows, (db + 1) * aux_rows, dims, "ret_aux", True, True)
            o_r_aux = o_r_aux.reshape(db + 1, aux_rows, rv_w)
            o_r, s_main = _ret_call(qkv, qkv, qkv, qkv,
                                    (c_qr, c_kr, c_vr, c_gr),
                                    s_aux[0:1], dec_m, qd_m, kd_m[None], sd_m[None],
                                    b, seq // BLOCK, BLOCK, n_main, dims, "ret_prompt", False, False)
            o_r_aux = pad_aux([o_r_aux[0, :n_meta], o_r_aux[1:, :n_new].reshape(db * n_new, rv_w)], rv_w)

            w_out = att_w_out[a].astype(BF16)
            x = _mm_res_call([o_a, o_r], [o_a_aux, o_r_aux], [w_out[:sb_w], w_out[sb_w:]],
                             jnp.zeros((d,), F32), x, "w_out")

            def with_meta(main_cols, meta_rows):
                m = jnp.broadcast_to(meta_rows.reshape(1, n_meta, sb_heads, sb_hd), (b, n_meta, sb_heads, sb_hd))
                return jnp.concatenate([m, main_cols.reshape(b, seq, sb_heads, sb_hd)], axis=1)

            k_p.append(with_meta(qkv[:n_main, c_k:c_k + sb_w], meta_k))
            v_p.append(with_meta(qkv[:n_main, c_v:c_v + sb_w], meta_v))
            ret_p.append(s_main)
            k_s.append(samp_k.reshape(db, n_new, sb_heads, sb_hd))
            v_s.append(samp_v.reshape(db, n_new, sb_heads, sb_hd))
            ret_s.append(s_aux[1:])
        else:
            c = layer // 2
            u = _pw1_call(x, norm_mix[layer], conv_w_pw1[c].astype(BF16), conv_b_pw1[c])
            u_meta = u[r_meta:r_meta + n_meta]
            u_samp = u[r_samp:r_samp + db * n_new].reshape(db, n_new, conv_w)
            first = jnp.concatenate([jnp.zeros((CONV_HALO - n_meta, conv_w), F32), u_meta], axis=0)
            w_dw = conv_w_dw[c].astype(BF16).astype(F32)
            cm = _conv_main_call(u, first, w_dw, conv_b_dw[c], conv_ln_g[c], conv_ln_b[c], dims)
            cur_aux = jnp.concatenate(
                [u_meta[None], jnp.concatenate([u_samp, jnp.zeros((db, n_meta - n_new, conv_w), F32)], axis=1)],
                axis=0)
            hist = state_conv[c].astype(F32)
            prev_aux = jnp.concatenate(
                [jnp.zeros((1, CONV_HALO, conv_w), F32),
                 jnp.concatenate([jnp.zeros((db, CONV_HALO - hist.shape[1], conv_w), F32), hist], axis=1)], axis=0)
            ca = _conv_aux_call(cur_aux, prev_aux, w_dw, conv_b_dw[c], conv_ln_g[c], conv_ln_b[c])
            cm_aux = pad_aux([ca[0], ca[1:, :n_new].reshape(db * n_new, conv_w)], conv_w)
            x = _mm_res_call([cm], [cm_aux], [conv_w_pw2[c].astype(BF16)], conv_b_pw2[c], x, "pw2")
            keep = hist.shape[1]
            conv_p.append(u[:n_main].reshape(b, seq, conv_w)[:, seq - keep:])
            conv_s.append(jnp.concatenate([hist, u_samp], axis=1)[:, n_new:].astype(state_conv.dtype))
        x = _moe_layer(x, norm_ffn[layer], w_router[layer], moe_w_gate, moe_w_up, moe_w_down, layer, n_real,
                       n_groups, per_group)

    return (x[:n_main].reshape(b, seq, d), x[r_samp:r_samp + db * n_new].reshape(db, n_new, d),
            jnp.stack(k_p), jnp.stack(v_p), jnp.stack(ret_p), jnp.stack(conv_p),
            jnp.stack(k_s), jnp.stack(v_s), jnp.stack(ret_s), jnp.stack(conv_s))
```

```python
import functools

import numpy as np
import jax
import jax.numpy as jnp
from jax import lax
from jax.experimental import pallas as pl
from jax.experimental.pallas import tpu as pltpu

F32 = jnp.float32
BF16 = jnp.bfloat16

EPS = 1e-6
BLOCK = 128
ROPE_BASE = 10000.0
TOP_K = 2
LANES = 128
SUBLANES = 8
TM = 512
TN = 512
TN_WIDE = 1024
MOE_HC = 256
ATT_HPS = 8
ATT_SPS = 8
RET_HPS = 2
MOE_TM = 256
DMA_UNROLL = 8
CONV_TM = 128
CONV_HALO = 32
CONV_LANES = 256
VMEM_LIMIT = 56 * 1024 * 1024

_NT_DIMS = (((1,), (1,)), ((), ()))


def _params(sem, vmem=VMEM_LIMIT):
    return pltpu.CompilerParams(dimension_semantics=sem, vmem_limit_bytes=vmem)


def _store_normed(x_ref, g_ref, xn_ref):
    x = x_ref[...]
    ms = jnp.mean(x * x, axis=-1, keepdims=True)
    xn_ref[...] = (x * lax.rsqrt(ms + EPS) * g_ref[...]).astype(xn_ref.dtype)


def _win_kernel(x_ref, g_ref, w_ref, qg_ref, kg_ref, cos_ref, sin_ref, o_ref, xn_ref, *,
                t_q, t_k, t_rq, t_rk, t_rend, ret_scale):
    j = pl.program_id(1)

    @pl.when(j == 0)
    def _():
        _store_normed(x_ref, g_ref, xn_ref)

    acc = jnp.dot(xn_ref[...], w_ref[...], preferred_element_type=F32)
    tn = acc.shape[1]
    is_qk = j < t_k
    is_rot = (j >= t_rq) & (j < t_rend)

    @pl.when(is_qk)
    def _():
        gain = jnp.where(j < t_q, qg_ref[...], kg_ref[...])
        for g in range(tn // LANES):
            a = acc[:, g * LANES:(g + 1) * LANES]
            ms = jnp.mean(a * a, axis=-1, keepdims=True)
            o_ref[:, g * LANES:(g + 1) * LANES] = a * lax.rsqrt(ms + EPS) * gain

    @pl.when(is_rot)
    def _():
        scale = jnp.where(j < t_rk, ret_scale, 1.0).astype(F32)
        cos = cos_ref[...]
        sin = sin_ref[...]
        half = cos.shape[1]
        for hd in range(tn // (2 * half)):
            lo = hd * 2 * half
            x1 = acc[:, lo:lo + half]
            x2 = acc[:, lo + half:lo + 2 * half]
            o_ref[:, lo:lo + half] = (x1 * cos - x2 * sin) * scale
            o_ref[:, lo + half:lo + 2 * half] = (x1 * sin + x2 * cos) * scale

    @pl.when(jnp.logical_not(is_qk | is_rot))
    def _():
        o_ref[...] = acc


def _w_in_call(x, g, w_bf, q_gain, k_gain, cos_t, sin_t, dims):
    nt, d = x.shape
    n = w_bf.shape[1]
    sb_w, rqk_w = dims["sb_width"], dims["ret_qk_width"]
    half = dims["ret_qk_dim"] // 2
    tn = min(TN_WIDE, sb_w, rqk_w)
    t_q = sb_w // tn
    t_k = 2 * sb_w // tn
    t_rq = 3 * sb_w // tn
    t_rk = t_rq + rqk_w // tn
    t_rend = t_rk + rqk_w // tn
    kern = functools.partial(_win_kernel, t_q=t_q, t_k=t_k, t_rq=t_rq, t_rk=t_rk, t_rend=t_rend,
                             ret_scale=float(dims["ret_qk_dim"]) ** -0.5)
    return pl.pallas_call(
        kern,
        out_shape=jax.ShapeDtypeStruct((nt, n), F32),
        grid=(nt // TM, n // tn),
        in_specs=[
            pl.BlockSpec((TM, d), lambda i, j: (i, 0)),
            pl.BlockSpec((1, d), lambda i, j: (0, 0)),
            pl.BlockSpec((d, tn), lambda i, j: (0, j)),
            pl.BlockSpec((1, LANES), lambda i, j: (0, 0)),
            pl.BlockSpec((1, LANES), lambda i, j: (0, 0)),
            pl.BlockSpec((TM, half), lambda i, j: (i, 0)),
            pl.BlockSpec((TM, half), lambda i, j: (i, 0)),
        ],
        out_specs=pl.BlockSpec((TM, tn), lambda i, j: (i, j)),
        scratch_shapes=[pltpu.VMEM((TM, d), BF16)],
        compiler_params=_params(("parallel", "arbitrary")),
        name="w_in",
    )(x, g.reshape(1, d), w_bf, q_gain.reshape(1, -1), k_gain.reshape(1, -1), cos_t, sin_t)


def _pw1_kernel(x_ref, g_ref, wv_ref, wg_ref, bv_ref, bg_ref, o_ref, xn_ref):
    @pl.when(pl.program_id(1) == 0)
    def _():
        _store_normed(x_ref, g_ref, xn_ref)

    xn = xn_ref[...]
    val = jnp.dot(xn, wv_ref[...], preferred_element_type=F32) + bv_ref[...]
    gate = jnp.dot(xn, wg_ref[...], preferred_element_type=F32) + bg_ref[...]
    o_ref[...] = val * jax.nn.sigmoid(gate)


def _pw1_call(x, g, w_bf, b):
    nt, d = x.shape
    c = w_bf.shape[1] // 2
    tn = min(TN_WIDE, c)
    nj = c // tn
    b2 = b.reshape(1, 2 * c)
    return pl.pallas_call(
        _pw1_kernel,
        out_shape=jax.ShapeDtypeStruct((nt, c), F32),
        grid=(nt // TM, nj),
        in_specs=[
            pl.BlockSpec((TM, d), lambda i, j: (i, 0)),
            pl.BlockSpec((1, d), lambda i, j: (0, 0)),
            pl.BlockSpec((d, tn), lambda i, j: (0, j)),
            pl.BlockSpec((d, tn), lambda i, j: (0, j + nj)),
            pl.BlockSpec((1, tn), lambda i, j: (0, j)),
            pl.BlockSpec((1, tn), lambda i, j: (0, j + nj)),
        ],
        out_specs=pl.BlockSpec((TM, tn), lambda i, j: (i, j)),
        scratch_shapes=[pltpu.VMEM((TM, d), BF16)],
        compiler_params=_params(("parallel", "arbitrary")),
        name="pw1_glu",
    )(x, g.reshape(1, d), w_bf, w_bf, b2, b2)


def _mm_res_kernel(*refs, n_lhs):
    main_refs = refs[:n_lhs]
    aux_refs = refs[n_lhs:2 * n_lhs]
    w_refs = refs[2 * n_lhs:3 * n_lhs]
    b_ref, r_ref, o_ref = refs[3 * n_lhs:3 * n_lhs + 3]
    abf_refs = refs[3 * n_lhs + 3:]
    i = pl.program_id(0)
    is_aux = i == pl.num_programs(0) - 1

    @pl.when((pl.program_id(1) == 0) & jnp.logical_not(is_aux))
    def _():
        for a_ref, abf_ref in zip(main_refs, abf_refs):
            abf_ref[...] = a_ref[...].astype(BF16)

    @pl.when((pl.program_id(1) == 0) & is_aux)
    def _():
        for a_ref, abf_ref in zip(aux_refs, abf_refs):
            abf_ref[...] = a_ref[...].astype(BF16)

    acc = jnp.dot(abf_refs[0][...], w_refs[0][...], preferred_element_type=F32)
    for abf_ref, w_ref in zip(abf_refs[1:], w_refs[1:]):
        acc = acc + jnp.dot(abf_ref[...], w_ref[...], preferred_element_type=F32)
    o_ref[...] = r_ref[...] + (acc + b_ref[...])


def _mm_res_call(lhs_list, aux_list, w_list, bias, res, name):
    nt, d = res.shape
    n_lhs = len(lhs_list)
    n_main_tiles = nt // TM - 1
    in_specs = []
    for a in lhs_list:
        in_specs.append(pl.BlockSpec((TM, a.shape[1]), lambda i, j: (jnp.minimum(i, n_main_tiles - 1), 0)))
    for a in aux_list:
        in_specs.append(pl.BlockSpec((TM, a.shape[1]), lambda i, j: (0, 0)))
    for w in w_list:
        in_specs.append(pl.BlockSpec((w.shape[0], TN), lambda i, j: (0, j)))
    in_specs.append(pl.BlockSpec((1, TN), lambda i, j: (0, j)))
    in_specs.append(pl.BlockSpec((TM, TN), lambda i, j: (i, j)))
    return pl.pallas_call(
        functools.partial(_mm_res_kernel, n_lhs=n_lhs),
        out_shape=jax.ShapeDtypeStruct((nt, d), F32),
        grid=(nt // TM, d // TN),
        in_specs=in_specs,
        out_specs=pl.BlockSpec((TM, TN), lambda i, j: (i, j)),
        scratch_shapes=[pltpu.VMEM((TM, a.shape[1]), BF16) for a in lhs_list],
        compiler_params=_params(("parallel", "arbitrary")),
        name=name,
    )(*lhs_list, *aux_list, *w_list, bias.reshape(1, d), res)


def _router_kernel(x_ref, g_ref, w_ref, hn_ref, ids_ref, gates_ref, *, n_groups, per_group):
    x = x_ref[...]
    ms = jnp.mean(x * x, axis=-1, keepdims=True)
    hn = x * lax.rsqrt(ms + EPS) * g_ref[...]
    hn_ref[...] = hn
    lg = jnp.dot(hn.astype(BF16), w_ref[...].astype(BF16), preferred_element_type=F32)
    lane = lax.broadcasted_iota(jnp.int32, lg.shape, 1).astype(F32)
    neg = -jnp.inf
    far = float(LANES)

    def first_argmax(v, vmax):
        return jnp.min(jnp.where(v == vmax, lane, far), axis=-1, keepdims=True)

    gl = jnp.where(lane < n_groups, lg, neg)
    p = jnp.exp(gl - jnp.max(gl, axis=-1, keepdims=True))
    prob = p / jnp.sum(p, axis=-1, keepdims=True)
    g_val = jnp.max(prob, axis=-1, keepdims=True)
    g_idx = first_argmax(prob, g_val)
    lo = n_groups + g_idx * per_group
    el = jnp.where((lane >= lo) & (lane < lo + per_group), lg, neg)
    e1 = jnp.max(el, axis=-1, keepdims=True)
    i1 = first_argmax(el, e1)
    el2 = jnp.where(lane == i1, neg, el)
    e2 = jnp.max(el2, axis=-1, keepdims=True)
    i2 = first_argmax(el2, e2)
    t = jnp.exp(e2 - e1)
    den = 1.0 + t
    ids_ref[...] = jnp.where(lane == 0, i1 - n_groups, jnp.where(lane == 1, i2 - n_groups, 0.0)).astype(jnp.int32)
    gates_ref[...] = jnp.where(lane == 0, g_val * (1.0 / den), jnp.where(lane == 1, g_val * (t / den), 0.0))


def _router_call(x, g, w_router, n_groups, per_group):
    nt, d = x.shape
    tm = TM // 2
    return pl.pallas_call(
        functools.partial(_router_kernel, n_groups=n_groups, per_group=per_group),
        out_shape=(jax.ShapeDtypeStruct((nt, d), F32), jax.ShapeDtypeStruct((nt, LANES), jnp.int32),
                   jax.ShapeDtypeStruct((nt, LANES), F32)),
        grid=(nt // tm,),
        in_specs=[
            pl.BlockSpec((tm, d), lambda i: (i, 0)),
            pl.BlockSpec((1, d), lambda i: (0, 0)),
            pl.BlockSpec((d, LANES), lambda i: (0, 0)),
        ],
        out_specs=(pl.BlockSpec((tm, d), lambda i: (i, 0)), pl.BlockSpec((tm, LANES), lambda i: (i, 0)),
                   pl.BlockSpec((tm, LANES), lambda i: (i, 0))),
        compiler_params=_params(("parallel",)),
        name="router",
    )(x, g.reshape(1, d), w_router)


def _sb_weights_many(zs, mask, carries, tri):
    sps = [jnp.maximum(z, 0.0) + jnp.log1p(jnp.exp(-jnp.abs(z))) for z in zs]
    log_nots = [-sp for sp in sps]
    if mask is not None:
        log_nots = [jnp.where(mask, ln, 0.0) for ln in log_nots]
    his = [ln.astype(BF16) for ln in log_nots]
    r1s = [ln - hi.astype(F32) for ln, hi in zip(log_nots, his)]
    mids = [r1.astype(BF16) for r1 in r1s]
    los = [(r1 - mid.astype(F32)).astype(BF16) for r1, mid in zip(r1s, mids)]
    c_hi = [jnp.dot(p, tri, preferred_element_type=F32) for p in his]
    c_mid = [jnp.dot(p, tri, preferred_element_type=F32) for p in mids]
    c_lo = [jnp.dot(p, tri, preferred_element_type=F32) for p in los]
    ws = []
    for z, sp, a, b, c, carry in zip(zs, sps, c_hi, c_mid, c_lo, carries):
        w = jnp.exp((z - sp) + (((a + b) + c) + carry))
        ws.append(w if mask is None else jnp.where(mask, w, 0.0))
    new_carries = [carry + jnp.sum(ln, axis=-1, keepdims=True) for carry, ln in zip(carries, log_nots)]
    return ws, new_carries


def _attn_prompt_kernel(bias_ref, q_ref, k_ref, v_ref, mk_ref, mv_ref, tri_ref, o_ref, carry_ref, acc_ref, *,
                        n_meta, has_main, scale, hd):
    hg = pl.program_id(1)
    i = pl.program_id(2)
    tri = tri_ref[...]
    nq = q_ref.shape[0]
    n_h = q_ref.shape[1] // hd
    row = lax.broadcasted_iota(jnp.int32, (nq, BLOCK), 0)
    col = lax.broadcasted_iota(jnp.int32, (nq, BLOCK), 1)
    qs = [q_ref[:, j * hd:(j + 1) * hd].astype(BF16) for j in range(n_h)]
    biases = [bias_ref[hg * n_h + j] for j in range(n_h)]

    def step(kv_rows, kr, vr, mask):
        zs = [lax.dot_general(qs[j], kr[kv_rows, j * hd:(j + 1) * hd].astype(BF16), _NT_DIMS,
                              preferred_element_type=F32) * scale + biases[j] for j in range(n_h)]
        ws, carries = _sb_weights_many(zs, mask, [carry_ref[j] for j in range(n_h)], tri)
        pvs = [jnp.dot(ws[j].astype(BF16), vr[kv_rows, j * hd:(j + 1) * hd].astype(BF16),
                       preferred_element_type=F32) for j in range(n_h)]
        for j in range(n_h):
            carry_ref[j] = carries[j]
            acc_ref[j] += pvs[j]

    carry_ref[...] = jnp.zeros_like(carry_ref)
    acc_ref[...] = jnp.zeros_like(acc_ref)
    if has_main:
        step(pl.ds(pl.multiple_of(i * BLOCK, BLOCK), BLOCK), k_ref, v_ref, col < row)

        def body(n, _):
            step(pl.ds(pl.multiple_of((i - 1 - n) * BLOCK, BLOCK), BLOCK), k_ref, v_ref, None)
            return 0

        lax.fori_loop(0, i, body, 0)
        meta_mask = col < n_meta
    else:
        meta_mask = (col < n_meta) & (col < row)
    step(slice(None), mk_ref, mv_ref, meta_mask)
    for j in range(n_h):
        o_ref[:, j * hd:(j + 1) * hd] = acc_ref[j]


def _attn_prompt_call(qkv, sb_bias, tri, dims, has_main):
    b, seq, h, hd = dims["batch"], dims["seq"], dims["sb_heads"], dims["sb_head_dim"]
    nq = seq // BLOCK
    aux_blk = (b * seq) // BLOCK
    hw = ATT_HPS * hd
    n_hg = h // ATT_HPS
    kern = functools.partial(_attn_prompt_kernel, n_meta=dims["n_meta"], has_main=has_main,
                             scale=float(hd) ** -0.5, hd=hd)
    if has_main:
        grid = (b, n_hg, nq)
        q_map = lambda bb, hh, ii: (bb * nq + ii, hh)
        out_rows = b * seq
    else:
        grid = (1, n_hg, 1)
        q_map = lambda bb, hh, ii: (aux_blk, hh)
        out_rows = BLOCK
    o_map = q_map if has_main else (lambda bb, hh, ii: (0, hh))
    return pl.pallas_call(
        kern,
        out_shape=jax.ShapeDtypeStruct((out_rows, h * hd), F32),
        grid=grid,
        in_specs=[
            pl.BlockSpec(memory_space=pltpu.SMEM),
            pl.BlockSpec((BLOCK, hw), q_map),
            pl.BlockSpec((seq, hw), lambda bb, hh, ii: (bb, n_hg + hh)),
            pl.BlockSpec((seq, hw), lambda bb, hh, ii: (bb, 2 * n_hg + hh)),
            pl.BlockSpec((BLOCK, hw), lambda bb, hh, ii: (aux_blk, n_hg + hh)),
            pl.BlockSpec((BLOCK, hw), lambda bb, hh, ii: (aux_blk, 2 * n_hg + hh)),
            pl.BlockSpec((BLOCK, BLOCK), lambda bb, hh, ii: (0, 0)),
        ],
        out_specs=pl.BlockSpec((BLOCK, hw), o_map),
        scratch_shapes=[pltpu.VMEM((ATT_HPS, BLOCK, 1), F32), pltpu.VMEM((ATT_HPS, BLOCK, hd), F32)],
        compiler_params=_params(("parallel", "parallel", "arbitrary")),
        name="attn_prompt" if has_main else "attn_meta",
    )(sb_bias.astype(F32), qkv, qkv, qkv, qkv, qkv, tri)


def _attn_sample_kernel(pt_ref, q_ref, bias_ref, kn_ref, vn_ref, *rest, n_heads, n_new, n_seq, scale):
    kp_refs = rest[:n_seq]
    vp_refs = rest[n_seq:2 * n_seq]
    tri_ref, o_ref, carry_ref, acc_ref = rest[2 * n_seq:]
    j = pl.program_id(1)
    last = pl.num_programs(1) - 1
    tri = tri_ref[...]
    nrow = q_ref.shape[1]

    def process(kf_refs, vf_refs, mask):
        zs = []
        for s in range(n_seq):
            q = q_ref[s]
            zh = []
            for hh in range(n_heads):
                k_h = kf_refs[s][pl.ds(hh, BLOCK, stride=n_heads), :].astype(BF16)
                zh.append(lax.dot_general(q[hh * n_new:(hh + 1) * n_new].astype(BF16), k_h, _NT_DIMS,
                                          preferred_element_type=F32))
            zs.append(jnp.concatenate(zh, axis=0) * scale + bias_ref[...])
        ws, carries = _sb_weights_many(zs, mask, [carry_ref[s] for s in range(n_seq)], tri)
        for s in range(n_seq):
            outs = []
            for hh in range(n_heads):
                v_h = vf_refs[s][pl.ds(hh, BLOCK, stride=n_heads), :].astype(BF16)
                outs.append(jnp.dot(ws[s][hh * n_new:(hh + 1) * n_new].astype(BF16), v_h,
                                    preferred_element_type=F32))
            carry_ref[s] = carries[s]
            acc_ref[s] += jnp.concatenate(outs, axis=0)

    @pl.when(j == 0)
    def _():
        carry_ref[...] = jnp.zeros_like(carry_ref)
        acc_ref[...] = jnp.zeros_like(acc_ref)
        row = lax.broadcasted_iota(jnp.int32, (nrow, BLOCK), 0)
        col = lax.broadcasted_iota(jnp.int32, (nrow, BLOCK), 1)
        process([kn_ref.at[s] for s in range(n_seq)], [vn_ref.at[s] for s in range(n_seq)],
                col < (row % n_new))

    @pl.when(j > 0)
    def _():
        process(kp_refs, vp_refs, None)

    @pl.when(j == last)
    def _():
        o_ref[...] = acc_ref[...]


def _attn_sample_call(page_table, q_s, bias_col, k_new, v_new, cache_k4, cache_v4, tri, layer, dims):
    db, n_pages = page_table.shape
    h, hd, n_new = dims["sb_heads"], dims["sb_head_dim"], dims["dec_seq"]
    rows = cache_k4.shape[2]
    n_seq = ATT_SPS if db % ATT_SPS == 0 else 1
    kern = functools.partial(_attn_sample_kernel, n_heads=h, n_new=n_new, n_seq=n_seq, scale=float(hd) ** -0.5)

    def page_map(s):
        return lambda g, jj, pt: (layer, pt[(g * n_seq + s) * n_pages + n_pages - jnp.maximum(jj, 1)], 0, 0)

    page_specs = [pl.BlockSpec((None, None, rows, hd), page_map(s)) for s in range(n_seq)]
    gs = pltpu.PrefetchScalarGridSpec(
        num_scalar_prefetch=1,
        grid=(db // n_seq, n_pages + 1),
        in_specs=[
            pl.BlockSpec((n_seq, h * n_new, hd), lambda g, jj, pt: (g, 0, 0)),
            pl.BlockSpec((h * n_new, 1), lambda g, jj, pt: (0, 0)),
            pl.BlockSpec((n_seq, rows, hd), lambda g, jj, pt: (g, 0, 0)),
            pl.BlockSpec((n_seq, rows, hd), lambda g, jj, pt: (g, 0, 0)),
            *page_specs, *page_specs,
            pl.BlockSpec((BLOCK, BLOCK), lambda g, jj, pt: (0, 0)),
        ],
        out_specs=pl.BlockSpec((n_seq, h * n_new, hd), lambda g, jj, pt: (g, 0, 0)),
        scratch_shapes=[pltpu.VMEM((n_seq, h * n_new, 1), F32), pltpu.VMEM((n_seq, h * n_new, hd), F32)],
    )
    return pl.pallas_call(
        kern,
        out_shape=jax.ShapeDtypeStruct((db, h * n_new, hd), F32),
        grid_spec=gs,
        compiler_params=_params(("parallel", "arbitrary")),
        name="attn_sample",
    )(page_table.reshape(-1), q_s, bias_col, k_new, v_new, *([cache_k4] * n_seq), *([cache_v4] * n_seq), tri)


def _ret_kernel(q_ref, k_ref, v_ref, g_ref, s0_ref, dec_ref, qd_ref, kd_ref, sd_ref, o_ref, sout_ref, state_ref, *,
                dk, dv):
    c = pl.program_id(2)
    n_h = state_ref.shape[0]
    hs = range(n_h)

    @pl.when(c == 0)
    def _():
        state_ref[...] = s0_ref[...]

    ks = [k_ref[:, j * dk:(j + 1) * dk] for j in hs]
    qbs = [q_ref[:, j * dk:(j + 1) * dk].astype(BF16) for j in hs]
    vbs = [v_ref[:, j * dv:(j + 1) * dv].astype(BF16) for j in hs]
    states = [state_ref[j] for j in hs]
    scores = [lax.dot_general(qbs[j], ks[j].astype(BF16), _NT_DIMS, preferred_element_type=F32) * dec_ref[j]
              for j in hs]
    intra = [jnp.dot(scores[j].astype(BF16), vbs[j], preferred_element_type=F32) for j in hs]
    cross = [jnp.dot(qbs[j], states[j].astype(BF16), preferred_element_type=F32) * qd_ref[j] for j in hs]
    kds = [(ks[j] * kd_ref[j]).astype(BF16) for j in hs]
    upd = [lax.dot_general(kds[j], vbs[j], (((0,), (0,)), ((), ())), preferred_element_type=F32) for j in hs]
    for j in hs:
        state_ref[j] = sd_ref[j] * states[j] + upd[j]
        o = intra[j] + cross[j]
        mu = jnp.mean(o, axis=-1, keepdims=True)
        var = jnp.mean(jnp.square(o - mu), axis=-1, keepdims=True)
        o_ref[:, j * dv:(j + 1) * dv] = ((o - mu) * lax.rsqrt(var + EPS)) * jax.nn.silu(g_ref[:, j * dv:(j + 1) * dv])

    @pl.when(c == pl.num_programs(2) - 1)
    def _():
        sout_ref[...] = state_ref[...]


def _ret_call(q, k, v, g, col_off, s0, dec, qd, kd, sd, n_seq, n_chunks, rows, out_rows, dims, name,
              s0_per_seq, kd_per_seq):
    nh, dk, dv = dims["ret_heads"], dims["ret_qk_dim"], dims["ret_v_dim"]
    n_h = RET_HPS
    qo, ko, vo, go = col_off

    def rmap(off, width):
        return lambda s, hg, c: (s * n_chunks + c, off // (n_h * width) + hg)

    s0_map = (lambda s, hg, c: (s, hg, 0, 0)) if s0_per_seq else (lambda s, hg, c: (0, hg, 0, 0))
    kd_map = (lambda s, hg, c: (s, hg, 0, 0)) if kd_per_seq else (lambda s, hg, c: (0, hg, 0, 0))
    return pl.pallas_call(
        functools.partial(_ret_kernel, dk=dk, dv=dv),
        out_shape=(jax.ShapeDtypeStruct((out_rows, nh * dv), F32),
                   jax.ShapeDtypeStruct((n_seq, nh, dk, dv), F32)),
        grid=(n_seq, nh // n_h, n_chunks),
        in_specs=[
            pl.BlockSpec((rows, n_h * dk), rmap(qo, dk)),
            pl.BlockSpec((rows, n_h * dk), rmap(ko, dk)),
            pl.BlockSpec((rows, n_h * dv), rmap(vo, dv)),
            pl.BlockSpec((rows, n_h * dv), rmap(go, dv)),
            pl.BlockSpec((None, n_h, dk, dv), s0_map),
            pl.BlockSpec((n_h, rows, rows), lambda s, hg, c: (hg, 0, 0)),
            pl.BlockSpec((n_h, rows, 1), lambda s, hg, c: (hg, 0, 0)),
            pl.BlockSpec((None, n_h, rows, 1), kd_map),
            pl.BlockSpec((None, n_h, 1, 1), kd_map),
        ],
        out_specs=(pl.BlockSpec((rows, n_h * dv), rmap(0, dv)),
                   pl.BlockSpec((None, n_h, dk, dv), lambda s, hg, c: (s, hg, 0, 0))),
        scratch_shapes=[pltpu.VMEM((n_h, dk, dv), F32)],
        compiler_params=_params(("parallel", "parallel", "arbitrary")),
        name=name,
    )(q, k, v, g, s0, dec, qd, kd, sd)


def _conv_kernel(prev_ref, first_ref, cur_ref, w_ref, b_ref, lg_ref, lb_ref, o_ref, ubuf_ref, qbuf_ref, *,
                 tiles_per_seq, width):
    tm, ch = cur_ref.shape
    if tiles_per_seq is None:
        halo = first_ref[...]
    else:
        is_first = (pl.program_id(0) % tiles_per_seq) == 0
        halo = jnp.where(is_first, first_ref[...], prev_ref[...])
    ubuf_ref[0:CONV_HALO, :] = halo.astype(BF16).astype(F32)
    ubuf_ref[CONV_HALO:CONV_HALO + tm, :] = cur_ref[...].astype(BF16).astype(F32)
    base = CONV_HALO - (width - 1)

    def chunk(cc, total):
        lo = pl.multiple_of(cc * CONV_LANES, CONV_LANES)
        lanes = pl.ds(lo, CONV_LANES)
        acc = None
        for rem in range(SUBLANES):
            taps = [kk for kk in range(width) if (base + kk) % SUBLANES == rem]
            if not taps:
                continue
            rows = tm if rem == 0 else tm + SUBLANES
            part = None
            for kk in taps:
                start = base + kk - rem
                term = w_ref[kk:kk + 1, lanes] * ubuf_ref[start:start + rows, lanes]
                part = term if part is None else part + term
            if rem:
                qbuf_ref[rem - 1, 0:rows, :] = part
                part = qbuf_ref[rem - 1, rem:rem + tm, :]
            acc = part if acc is None else acc + part
        acc = acc + b_ref[:, lanes]
        o_ref[:, pl.ds(lo, CONV_LANES)] = acc
        return total + jnp.sum(acc, axis=-1, keepdims=True)

    total = lax.fori_loop(0, ch // CONV_LANES, chunk, jnp.zeros((tm, 1), F32))
    mu = total / ch
    y = o_ref[...]
    var = jnp.mean(jnp.square(y - mu), axis=-1, keepdims=True)
    yn = (y - mu) * lax.rsqrt(var + EPS) * lg_ref[...] + lb_ref[...]
    o_ref[...] = jax.nn.silu(yn)


def _conv_main_call(u, first, w_dw, b_dw, ln_g, ln_b, dims):
    nt, ch = u.shape
    b, seq = dims["batch"], dims["seq"]
    tiles_per_seq = seq // CONV_TM
    width = w_dw.shape[0]
    ratio = CONV_TM // CONV_HALO
    kern = functools.partial(_conv_kernel, tiles_per_seq=tiles_per_seq, width=width)
    return pl.pallas_call(
        kern,
        out_shape=jax.ShapeDtypeStruct((b * seq, ch), F32),
        grid=(b * tiles_per_seq,),
        in_specs=[
            pl.BlockSpec((CONV_HALO, ch), lambda i: (jnp.maximum(i * ratio - 1, 0), 0)),
            pl.BlockSpec((CONV_HALO, ch), lambda i: (0, 0)),
            pl.BlockSpec((CONV_TM, ch), lambda i: (i, 0)),
            pl.BlockSpec((width, ch), lambda i: (0, 0)),
            pl.BlockSpec((1, ch), lambda i: (0, 0)),
            pl.BlockSpec((1, ch), lambda i: (0, 0)),
            pl.BlockSpec((1, ch), lambda i: (0, 0)),
        ],
        out_specs=pl.BlockSpec((CONV_TM, ch), lambda i: (i, 0)),
        scratch_shapes=[pltpu.VMEM((CONV_HALO + CONV_TM, ch), F32),
                        pltpu.VMEM((SUBLANES - 1, CONV_TM + SUBLANES, CONV_LANES), F32)],
        compiler_params=_params(("parallel",)),
        name="conv_main",
    )(u, first, u, w_dw, b_dw.reshape(1, ch), ln_g.reshape(1, ch), ln_b.reshape(1, ch))


def _conv_aux_call(cur, prev, w_dw, b_dw, ln_g, ln_b):
    n_seq, rows, ch = cur.shape
    width = w_dw.shape[0]
    kern = functools.partial(_conv_kernel, tiles_per_seq=None, width=width)
    return pl.pallas_call(
        kern,
        out_shape=jax.ShapeDtypeStruct((n_seq, rows, ch), F32),
        grid=(n_seq,),
        in_specs=[
            pl.BlockSpec((None, CONV_HALO, ch), lambda i: (i, 0, 0)),
            pl.BlockSpec((None, CONV_HALO, ch), lambda i: (i, 0, 0)),
            pl.BlockSpec((None, rows, ch), lambda i: (i, 0, 0)),
            pl.BlockSpec((width, ch), lambda i: (0, 0)),
            pl.BlockSpec((1, ch), lambda i: (0, 0)),
            pl.BlockSpec((1, ch), lambda i: (0, 0)),
            pl.BlockSpec((1, ch), lambda i: (0, 0)),
        ],
        out_specs=pl.BlockSpec((None, rows, ch), lambda i: (i, 0, 0)),
        scratch_shapes=[pltpu.VMEM((CONV_HALO + rows, ch), F32),
                        pltpu.VMEM((SUBLANES - 1, rows + SUBLANES, CONV_LANES), F32)],
        compiler_params=_params(("parallel",)),
        name="conv_aux",
    )(prev, prev, cur, w_dw, b_dw.reshape(1, ch), ln_g.reshape(1, ch), ln_b.reshape(1, ch))


def _moe_kernel(te_ref, tok_ref, dst_ref, nv_ref, nu_ref, hn_hbm, wg_ref, wu_ref, wd_ref, gw_ref, y_hbm,
                xbuf, ybuf, gsem, ssem, zsem, *, n_real, n_rows):
    t = pl.program_id(0)
    n_used = nu_ref[0]
    tm = ybuf.shape[0]
    slot = t % 2

    def gather_copy(tile, buf_slot, r):
        tok = tok_ref[tile * tm + r]
        return pltpu.make_async_copy(hn_hbm.at[pl.ds(tok, 1)], xbuf.at[buf_slot, pl.ds(r, 1)], gsem.at[buf_slot])

    def for_rows(n, fn):
        def group(gi, _):
            base = pl.multiple_of(gi * DMA_UNROLL, DMA_UNROLL)
            for u in range(DMA_UNROLL):
                fn(base + u)
            return 0
        lax.fori_loop(0, n // DMA_UNROLL, group, 0)

        def rest(r, _):
            fn(r)
            return 0
        lax.fori_loop((n // DMA_UNROLL) * DMA_UNROLL, n, rest, 0)

    def start_gather(tile, buf_slot):
        for_rows(tm, lambda r: gather_copy(tile, buf_slot, r).start())

    def wait_gather(buf_slot):
        pltpu.make_async_copy(hn_hbm.at[pl.ds(0, tm)], xbuf.at[buf_slot], gsem.at[buf_slot]).wait()

    def scatter_copy(tile, r):
        return pltpu.make_async_copy(ybuf.at[pl.ds(r, 1)], y_hbm.at[pl.ds(dst_ref[tile * tm + r], 1)], ssem)

    def start_scatter(tile):
        for_rows(nv_ref[tile], lambda r: scatter_copy(tile, r).start())

    def wait_scatter(tile):
        n = nv_ref[tile]

        def group(gi, _):
            pltpu.make_async_copy(ybuf.at[pl.ds(0, DMA_UNROLL)], y_hbm.at[pl.ds(0, DMA_UNROLL)], ssem).wait()
            return 0
        lax.fori_loop(0, n // DMA_UNROLL, group, 0)

        def rest(r, _):
            scatter_copy(tile, r).wait()
            return 0
        lax.fori_loop((n // DMA_UNROLL) * DMA_UNROLL, n, rest, 0)

    @pl.when(t == 0)
    def _():
        ybuf[...] = jnp.zeros_like(ybuf)
        n_pad = n_rows - n_real
        copies = []
        for s in range(TOP_K):
            done = 0
            while done < n_pad:
                cnt = min(tm, n_pad - done)
                copies.append(pltpu.make_async_copy(ybuf.at[pl.ds(0, cnt)],
                                                    y_hbm.at[pl.ds(s * n_rows + n_real + done, cnt)], zsem))
                done += cnt
        for cp in copies:
            cp.start()
        for cp in copies:
            cp.wait()
        start_gather(0, 0)

    @pl.when(t < n_used)
    def _():
        wait_gather(slot)

        @pl.when(t + 1 < n_used)
        def _():
            start_gather(t + 1, 1 - slot)

        x = xbuf[slot].astype(BF16)
        n_hid = wg_ref.shape[1]
        hc = min(MOE_HC, n_hid)
        y = None
        for c in range(n_hid // hc):
            gate = jnp.dot(x, wg_ref[:, c * hc:(c + 1) * hc].astype(BF16), preferred_element_type=F32)
            up = jnp.dot(x, wu_ref[:, c * hc:(c + 1) * hc].astype(BF16), preferred_element_type=F32)
            hid = (jax.nn.silu(gate) * up).astype(BF16)
            part = jnp.dot(hid, wd_ref[c * hc:(c + 1) * hc, :].astype(BF16), preferred_element_type=F32)
            y = part if y is None else y + part
        y = y * gw_ref[...]

        @pl.when(t > 0)
        def _():
            wait_scatter(t - 1)

        ybuf[...] = y
        start_scatter(t)

        @pl.when(t == n_used - 1)
        def _():
            wait_scatter(t)


def _moe_call(hn, tile_expert, tok_pad, dst_pad, n_valid, n_used, gate_w, w_gate, w_up, w_down, layer, n_real):
    nt, d = hn.shape
    cap = tok_pad.shape[0]
    n_tiles = cap // MOE_TM
    hid = w_gate.shape[-1]
    kern = functools.partial(_moe_kernel, n_real=n_real, n_rows=nt)
    gs = pltpu.PrefetchScalarGridSpec(
        num_scalar_prefetch=5,
        grid=(n_tiles,),
        in_specs=[
            pl.BlockSpec(memory_space=pl.ANY),
            pl.BlockSpec((None, None, d, hid), lambda t, te, *_: (layer, te[t], 0, 0)),
            pl.BlockSpec((None, None, d, hid), lambda t, te, *_: (layer, te[t], 0, 0)),
            pl.BlockSpec((None, None, hid, d), lambda t, te, *_: (layer, te[t], 0, 0)),
            pl.BlockSpec((MOE_TM, 1), lambda t, *_: (t, 0)),
        ],
        out_specs=pl.BlockSpec(memory_space=pl.ANY),
        scratch_shapes=[
            pltpu.VMEM((2, MOE_TM, d), F32),
            pltpu.VMEM((MOE_TM, d), F32),
            pltpu.SemaphoreType.DMA((2,)),
            pltpu.SemaphoreType.DMA(()),
            pltpu.SemaphoreType.DMA(()),
        ],
    )
    return pl.pallas_call(
        kern,
        out_shape=jax.ShapeDtypeStruct((TOP_K * nt, d), F32),
        grid_spec=gs,
        compiler_params=_params(("arbitrary",)),
        name="moe_experts",
    )(tile_expert, tok_pad, dst_pad, n_valid, n_used, hn, w_gate, w_up, w_down, gate_w)


def _combine_kernel(x_ref, y0_ref, y1_ref, o_ref):
    o_ref[...] = x_ref[...] + (y0_ref[...] + y1_ref[...])


def _combine_call(x, y2):
    nt, d = x.shape
    nb = nt // TM
    return pl.pallas_call(
        _combine_kernel,
        out_shape=jax.ShapeDtypeStruct((nt, d), F32),
        grid=(nb,),
        in_specs=[
            pl.BlockSpec((TM, d), lambda i: (i, 0)),
            pl.BlockSpec((TM, d), lambda i: (i, 0)),
            pl.BlockSpec((TM, d), lambda i: (i + nb, 0)),
        ],
        out_specs=pl.BlockSpec((TM, d), lambda i: (i, 0)),
        compiler_params=_params(("parallel",)),
        name="moe_combine",
    )(x, y2, y2)


def _route(ids, gates, n_real, n_exp, nt):
    n_assign = n_real * TOP_K
    blk = MOE_TM
    n_fill = n_exp * (blk - 1)
    cap = (n_assign + n_fill + blk - 1) // blk * blk
    e_flat = ids[:n_real, :TOP_K].reshape(n_assign)
    w_flat = gates[:n_real, :TOP_K].reshape(n_assign)
    t_flat = jnp.repeat(jnp.arange(n_real, dtype=jnp.int32), TOP_K)
    s_flat = jnp.tile(jnp.arange(TOP_K, dtype=jnp.int32), n_real)
    counts = jnp.sum((e_flat[:, None] == jnp.arange(n_exp, dtype=jnp.int32)[None, :]).astype(jnp.int32), axis=0)
    padded = (counts + blk - 1) // blk * blk
    fill_key = jnp.where(jnp.arange(blk - 1, dtype=jnp.int32)[None, :] < (padded - counts)[:, None],
                         jnp.arange(n_exp, dtype=jnp.int32)[:, None], n_exp).reshape(n_fill)
    n_tail = cap - n_assign - n_fill
    keys = jnp.concatenate([e_flat, fill_key, jnp.full((n_tail,), n_exp, jnp.int32)])
    n_extra = n_fill + n_tail
    tok = jnp.concatenate([t_flat, jnp.full((n_extra,), n_real, jnp.int32)])
    dst = jnp.concatenate([s_flat * nt + t_flat, jnp.full((n_extra,), -1, jnp.int32)])
    wgt = jnp.concatenate([w_flat, jnp.zeros((n_extra,), F32)])
    keys_s, tok_pad, dst_pad, w_pad = lax.sort((keys, tok, dst, wgt), num_keys=1, is_stable=True)
    n_tiles = cap // blk
    tile_expert = jnp.minimum(keys_s[::blk], n_exp - 1)
    n_valid = jnp.sum((dst_pad >= 0).astype(jnp.int32).reshape(n_tiles, blk), axis=1)
    n_used = (jnp.sum(padded) // blk).astype(jnp.int32).reshape(1)
    return tile_expert, tok_pad, dst_pad, n_valid, n_used, w_pad.reshape(cap, 1)


def _moe_layer(x, g, w_router, w_gate, w_up, w_down, layer, n_real, n_groups, per_group):
    hn, ids, gates = _router_call(x, g, w_router, n_groups, per_group)
    tile_expert, tok_pad, dst_pad, n_valid, n_used, gate_w = _route(ids, gates, n_real, n_groups * per_group,
                                                                   x.shape[0])
    y2 = _moe_call(hn, tile_expert, tok_pad, dst_pad, n_valid, n_used, gate_w, w_gate, w_up, w_down, layer, n_real)
    return _combine_call(x, y2)


def _decay_tables(n_heads, rows, live_rows):
    log_gamma = jnp.log1p(-jnp.exp2(-5.0 - jnp.arange(n_heads, dtype=F32)))
    idx = jnp.arange(rows, dtype=F32)
    rel = idx[:, None] - idx[None, :]
    dec = jnp.where(rel >= 0, jnp.exp(jnp.maximum(rel, 0.0)[None] * log_gamma[:, None, None]), 0.0)
    qd = jnp.exp((idx[None, :] + 1.0) * log_gamma[:, None])[:, :, None]
    kd = jnp.exp((live_rows - 1.0 - idx)[None, :] * log_gamma[:, None])
    kd = jnp.where(idx[None, :] < live_rows, kd, 0.0)[:, :, None]
    sd = jnp.exp(live_rows * log_gamma)[:, None, None]
    return dec, qd, kd, sd


def kernel(x_prompt, x_sample, cache_k, cache_v, state_ret, state_conv, page_table, meta_tokens, norm_mix, norm_ffn, att_w_in, att_q_gain, att_k_gain, att_sb_bias, att_w_out, conv_w_pw1, conv_b_pw1, conv_w_dw, conv_b_dw, conv_ln_g, conv_ln_b, conv_w_pw2, conv_b_pw2, moe_router_group, moe_router_expert, moe_w_gate, moe_w_up, moe_w_down):
    b, seq, d = x_prompt.shape
    db, n_new, _ = x_sample.shape
    n_meta = meta_tokens.shape[0]
    depth = norm_mix.shape[0]
    _, n_pool, page, sb_heads, sb_hd = cache_k.shape
    _, _, ret_heads, ret_dk, ret_dv = state_ret.shape
    n_groups, per_group = moe_router_expert.shape[2:]
    past_len = page_table.shape[1] * page
    sb_w = sb_heads * sb_hd
    rqk_w = ret_heads * ret_dk
    rv_w = ret_heads * ret_dv
    conv_w = conv_w_dw.shape[2]
    dims = dict(batch=b, seq=seq, n_meta=n_meta, sb_heads=sb_heads, sb_head_dim=sb_hd, sb_width=sb_w,
                ret_heads=ret_heads, ret_qk_dim=ret_dk, ret_v_dim=ret_dv, ret_qk_width=rqk_w,
                dec_seq=n_new)
    assert seq % TM == 0 and (b * seq) % TM == 0 and n_meta + db * n_new <= BLOCK
    assert page == BLOCK and sb_hd == LANES and ret_dk == 2 * LANES and n_meta % 8 == 0 and n_new == 8

    n_main = b * seq
    nt = n_main + TM
    r_meta = n_main
    r_samp = n_main + n_meta
    n_real = r_samp + db * n_new
    aux_pad = nt - n_real

    x = jnp.concatenate([x_prompt.reshape(n_main, d), meta_tokens.astype(F32), x_sample.reshape(db * n_new, d),
                         jnp.zeros((aux_pad, d), F32)], axis=0)

    pos = jnp.concatenate([jnp.tile(n_meta + jnp.arange(seq), b), jnp.arange(n_meta),
                           jnp.tile(past_len + jnp.arange(n_new), db), jnp.zeros((aux_pad,), jnp.int32)])
    half = ret_dk // 2
    inv_freq = ROPE_BASE ** (-jnp.arange(half, dtype=F32) / half)
    ang = pos.astype(F32)[:, None] * inv_freq[None, :]
    cos_t, sin_t = jnp.cos(ang), jnp.sin(ang)

    idx = jnp.arange(BLOCK)
    tri = (idx[:, None] > idx[None, :]).astype(BF16)

    dec_m, qd_m, kd_m, sd_m = _decay_tables(ret_heads, BLOCK, float(BLOCK))
    aux_rows = n_meta
    dec_a, qd_a, kd_meta, sd_meta = _decay_tables(ret_heads, aux_rows, float(n_meta))
    _, _, kd_s, sd_s = _decay_tables(ret_heads, aux_rows, float(n_new))
    kd_aux = jnp.concatenate([kd_meta[None], jnp.broadcast_to(kd_s[None], (db,) + kd_s.shape)], axis=0)
    sd_aux = jnp.concatenate([sd_meta[None], jnp.broadcast_to(sd_s[None], (db,) + sd_s.shape)], axis=0)

    cache_k4 = cache_k.reshape(cache_k.shape[0], n_pool, page * sb_heads, sb_hd)
    cache_v4 = cache_v.reshape(cache_v.shape[0], n_pool, page * sb_heads, sb_hd)

    w_router = jnp.concatenate(
        [moe_router_group, moe_router_expert.reshape(depth, d, n_groups * per_group),
         jnp.zeros((depth, d, LANES - n_groups - n_groups * per_group), F32)], axis=-1)

    def pad_aux(rows_list, width):
        used = sum(r.shape[0] for r in rows_list)
        return jnp.concatenate(list(rows_list) + [jnp.zeros((TM - used, width), F32)], axis=0)

    k_p, v_p, ret_p, conv_p, k_s, v_s, ret_s, conv_s = [], [], [], [], [], [], [], []
    for layer in range(depth):
        if layer % 2 == 0:
            a = layer // 2
            qkv = _w_in_call(x, norm_mix[layer], att_w_in[a].astype(BF16), att_q_gain[a], att_k_gain[a],
                             cos_t, sin_t, dims)
            c_k, c_v = sb_w, 2 * sb_w
            c_qr = 3 * sb_w
            c_kr = c_qr + rqk_w
            c_vr = c_kr + rqk_w
            c_gr = c_vr + rv_w
            aux = qkv[n_main:n_main + BLOCK]
            meta_k, meta_v = aux[:n_meta, c_k:c_k + sb_w], aux[:n_meta, c_v:c_v + sb_w]
            samp = aux[n_meta:n_meta + db * n_new]
            samp_k, samp_v = samp[:, c_k:c_k + sb_w], samp[:, c_v:c_v + sb_w]

            o_a = _attn_prompt_call(qkv, att_sb_bias[a], tri, dims, True)
            o_a_meta = _attn_prompt_call(qkv, att_sb_bias[a], tri, dims, False)[:n_meta]
            q_s = samp[:, :sb_w].reshape(db, n_new, sb_heads, sb_hd).transpose(0, 2, 1, 3)
            q_s = q_s.reshape(db, sb_heads * n_new, sb_hd)

            def new_page(rows):
                pg = rows.reshape(db, n_new * sb_heads, sb_hd)
                return jnp.concatenate([pg, jnp.zeros((db, (page - n_new) * sb_heads, sb_hd), F32)], axis=1)

            bias_col = jnp.repeat(att_sb_bias[a].astype(F32), n_new).reshape(sb_heads * n_new, 1)
            o_a_s = _attn_sample_call(page_table, q_s, bias_col, new_page(samp_k), new_page(samp_v),
                                      cache_k4, cache_v4, tri, a, dims)
            o_a_s = o_a_s.reshape(db, sb_heads, n_new, sb_hd).transpose(0, 2, 1, 3).reshape(db * n_new, sb_w)
            o_a_aux = pad_aux([o_a_meta, o_a_s], sb_w)

            def aux_seqs(lo, width):
                m = aux[:n_meta, lo:lo + width]
                m = jnp.concatenate([m, jnp.zeros((aux_rows - n_meta, width), F32)], axis=0)[None]
                s = samp[:, lo:lo + width].reshape(db, n_new, width)
                s = jnp.concatenate([s, jnp.zeros((db, aux_rows - n_new, width), F32)], axis=1)
                return jnp.concatenate([m, s], axis=0).reshape((db + 1) * aux_rows, width)

            s0_aux = jnp.concatenate([jnp.zeros((1, ret_heads, ret_dk, ret_dv), F32), state_ret[a]], axis=0)
            o_r_aux, s_aux = _ret_call(aux_seqs(c_qr, rqk_w), aux_seqs(c_kr, rqk_w), aux_seqs(c_vr, rv_w),
                                       aux_seqs(c_gr, rv_w), (0, 0, 0, 0), s0_aux, dec_a, qd_a, kd_aux, sd_aux,
                                       db + 1, 1, aux_rows, (db + 1) * aux_rows, dims, "ret_aux", True, True)
            o_r_aux = o_r_aux.reshape(db + 1, aux_rows, rv_w)
            o_r, s_main = _ret_call(qkv, qkv, qkv, qkv,
                                    (c_qr, c_kr, c_vr, c_gr),
                                    s_aux[0:1], dec_m, qd_m, kd_m[None], sd_m[None],
                                    b, seq // BLOCK, BLOCK, n_main, dims, "ret_prompt", False, False)
            o_r_aux = pad_aux([o_r_aux[0, :n_meta], o_r_aux[1:, :n_new].reshape(db * n_new, rv_w)], rv_w)

            w_out = att_w_out[a].astype(BF16)
            x = _mm_res_call([o_a, o_r], [o_a_aux, o_r_aux], [w_out[:sb_w], w_out[sb_w:]],
                             jnp.zeros((d,), F32), x, "w_out")

            def with_meta(main_cols, meta_rows):
                m = jnp.broadcast_to(meta_rows.reshape(1, n_meta, sb_heads, sb_hd), (b, n_meta, sb_heads, sb_hd))
                return jnp.concatenate([m, main_cols.reshape(b, seq, sb_heads, sb_hd)], axis=1)

            k_p.append(with_meta(qkv[:n_main, c_k:c_k + sb_w], meta_k))
            v_p.append(with_meta(qkv[:n_main, c_v:c_v + sb_w], meta_v))
            ret_p.append(s_main)
            k_s.append(samp_k.reshape(db, n_new, sb_heads, sb_hd))
            v_s.append(samp_v.reshape(db, n_new, sb_heads, sb_hd))
            ret_s.append(s_aux[1:])
        else:
            c = layer // 2
            u = _pw1_call(x, norm_mix[layer], conv_w_pw1[c].astype(BF16), conv_b_pw1[c])
            u_meta = u[r_meta:r_meta + n_meta]
            u_samp = u[r_samp:r_samp + db * n_new].reshape(db, n_new, conv_w)
            first = jnp.concatenate([jnp.zeros((CONV_HALO - n_meta, conv_w), F32), u_meta], axis=0)
            w_dw = conv_w_dw[c].astype(BF16).astype(F32)
            cm = _conv_main_call(u, first, w_dw, conv_b_dw[c], conv_ln_g[c], conv_ln_b[c], dims)
            cur_aux = jnp.concatenate(
                [u_meta[None], jnp.concatenate([u_samp, jnp.zeros((db, n_meta - n_new, conv_w), F32)], axis=1)],
                axis=0)
            hist = state_conv[c].astype(F32)
            prev_aux = jnp.concatenate(
                [jnp.zeros((1, CONV_HALO, conv_w), F32),
                 jnp.concatenate([jnp.zeros((db, CONV_HALO - hist.shape[1], conv_w), F32), hist], axis=1)], axis=0)
            ca = _conv_aux_call(cur_aux, prev_aux, w_dw, conv_b_dw[c], conv_ln_g[c], conv_ln_b[c])
            cm_aux = pad_aux([ca[0], ca[1:, :n_new].reshape(db * n_new, conv_w)], conv_w)
            x = _mm_res_call([cm], [cm_aux], [conv_w_pw2[c].astype(BF16)], conv_b_pw2[c], x, "pw2")
            keep = hist.shape[1]
            conv_p.append(u[:n_main].reshape(b, seq, conv_w)[:, seq - keep:])
            conv_s.append(jnp.concatenate([hist, u_samp], axis=1)[:, n_new:].astype(state_conv.dtype))
        x = _moe_layer(x, norm_ffn[layer], w_router[layer], moe_w_gate, moe_w_up, moe_w_down, layer, n_real,
                       n_groups, per_group)

    return (x[:n_main].reshape(b, seq, d), x[r_samp:r_samp + db * n_new].reshape(db, n_new, d),
            jnp.stack(k_p), jnp.stack(v_p), jnp.stack(ret_p), jnp.stack(conv_p),
            jnp.stack(k_s), jnp.stack(v_s), jnp.stack(ret_s), jnp.stack(conv_s))
```
